```python
import math
import jax, jax.numpy as jnp
from jax import lax
import numpy as np

D_MODEL = 4096
BATCH = 4
SEQ = 4096
DEPTH = 1

PLE_DIM = 256
CHUNK = 128
QBLOCK = 128
HEAD_DIM = 128
A_WIDTH = D_MODEL // 2
B_WIDTH = D_MODEL - A_WIDTH
SGU_HEADS = A_WIDTH // HEAD_DIM
SB_HEADS = B_WIDTH // HEAD_DIM
D_IN = 2 * A_WIDTH + 3 * B_WIDTH
D_FF = 4 * D_MODEL
EPS = 1e-6

kernel_name = "hybrid_sgu_stickbreaking_block"


def _rms_norm(x, g):
    x32 = x.astype(jnp.float32)
    y = x32 * lax.rsqrt(jnp.mean(x32 * x32, axis=-1, keepdims=True) + EPS)
    return (y * g.astype(jnp.float32)).astype(x.dtype)


def _head_rms_norm(y, g, n_heads):
    b, s, w = y.shape
    dh = w // n_heads
    y32 = y.reshape(b, s, n_heads, dh).astype(jnp.float32)
    y32 = y32 * lax.rsqrt(jnp.mean(y32 * y32, axis=-1, keepdims=True) + EPS)
    y32 = y32 * g.reshape(n_heads, dh).astype(jnp.float32)
    return y32.reshape(b, s, w).astype(y.dtype)


def _chunked_sgu(u, v, w_s, b_s):
    b, s, w = u.shape
    nc = s // CHUNK
    u = jax.nn.gelu(u)
    v = jax.nn.gelu(v)
    v32 = v.reshape(b, nc, CHUNK, SGU_HEADS, HEAD_DIM).astype(jnp.float32)
    mu = jnp.mean(v32, axis=-1, keepdims=True)
    var = jnp.mean(jnp.square(v32 - mu), axis=-1, keepdims=True)
    vn = (v32 - mu) * lax.rsqrt(var + EPS)
    causal = jnp.tril(jnp.ones((CHUNK, CHUNK), dtype=bool))
    w_masked = jnp.where(causal[None], w_s.astype(jnp.float32), 0.0)
    mixed = jnp.einsum('hts,bnshd->bnthd', w_masked, vn)
    mixed = mixed + b_s.astype(jnp.float32).T[None, None, :, :, None]
    return u * mixed.reshape(b, s, w).astype(u.dtype)


def _stick_breaking(q, k, v):
    b, s, w = q.shape
    nb = s // QBLOCK
    scale = 1.0 / math.sqrt(HEAD_DIM)

    def heads(t):
        return t.reshape(b, s, SB_HEADS, HEAD_DIM).transpose(0, 2, 1, 3).astype(jnp.float32)

    qh = heads(q) * scale
    kh = heads(k)
    vh = heads(v)
    q_blocks = qh.reshape(b, SB_HEADS, nb, QBLOCK, HEAD_DIM).transpose(2, 0, 1, 3, 4)
    key_pos = jnp.arange(s)

    def block(args):
        qb, bi = args
        z = jnp.einsum('bhqd,bhkd->bhqk', qb, kh)
        q_pos = bi * QBLOCK + jnp.arange(QBLOCK)
        causal = key_pos[None, :] < q_pos[:, None]
        log_keep = jnp.where(causal, jax.nn.log_sigmoid(-z), 0.0)
        log_after = lax.cumsum(log_keep, axis=3, reverse=True) - log_keep
        a = jnp.where(causal, jnp.exp(jax.nn.log_sigmoid(z) + log_after), 0.0)
        return jnp.einsum('bhqk,bhkd->bhqd', a, vh)

    o = lax.map(block, (q_blocks, jnp.arange(nb)))
    o = o.transpose(1, 0, 3, 2, 4).reshape(b, s, w)
    return o.astype(q.dtype)


def setup_inputs(seed: int = 0) -> dict:
    key = jax.random.key(seed)
    ks = jax.random.split(key, 18)
    f32 = jnp.float32

    def nrm(k, shape, scale):
        return jax.random.normal(k, shape, f32) * scale

    def gain(k, shape):
        return 1.0 + 0.02 * jax.random.normal(k, shape, f32)

    return {
        "x": nrm(ks[0], (BATCH, SEQ, D_MODEL), 1.0),
        "p": nrm(ks[1], (DEPTH, BATCH, SEQ, PLE_DIM), 1.0),
        "norm_mix": gain(ks[2], (DEPTH, D_MODEL)),
        "w_in": nrm(ks[3], (DEPTH, D_MODEL, D_IN), D_MODEL ** -0.5),
        "w_s": nrm(ks[4], (DEPTH, SGU_HEADS, CHUNK, CHUNK), CHUNK ** -0.5),
        "b_s": gain(ks[5], (DEPTH, SGU_HEADS, CHUNK)),
        "norm_a_out": gain(ks[6], (DEPTH, A_WIDTH)),
        "norm_b_out": gain(ks[7], (DEPTH, B_WIDTH)),
        "w_out": nrm(ks[8], (DEPTH, D_MODEL, D_MODEL), D_MODEL ** -0.5),
        "norm_ffn": gain(ks[9], (DEPTH, D_MODEL)),
        "w_up": nrm(ks[10], (DEPTH, D_MODEL, D_FF), D_MODEL ** -0.5),
        "w_down": nrm(ks[11], (DEPTH, D_FF, D_MODEL), D_FF ** -0.5),
        "norm_ple": gain(ks[12], (DEPTH, D_MODEL)),
        "w_ple_gate": nrm(ks[13], (DEPTH, D_MODEL, D_MODEL), D_MODEL ** -0.5),
        "w_ple_proj": nrm(ks[14], (DEPTH, PLE_DIM, D_MODEL), PLE_DIM ** -0.5),
        "norm_final": gain(ks[15], (D_MODEL,)),
    }


def reference(x, p, norm_mix, w_in, w_s, b_s, norm_a_out, norm_b_out, w_out,
              norm_ffn, w_up, w_down, norm_ple, w_ple_gate, w_ple_proj, norm_final):
    h = x
    for i in range(DEPTH):
        a = _rms_norm(h, norm_mix[i])
        z = jnp.einsum('bsd,de->bse', a, w_in[i])
        o1 = A_WIDTH
        o2 = 2 * A_WIDTH
        o3 = o2 + B_WIDTH
        o4 = o3 + B_WIDTH
        u_a, v_a = z[..., :o1], z[..., o1:o2]
        q_b, k_b, v_b = z[..., o2:o3], z[..., o3:o4], z[..., o4:]
        y_a = _chunked_sgu(u_a, v_a, w_s[i], b_s[i])
        y_b = _stick_breaking(q_b, k_b, v_b)
        y = jnp.concatenate([_head_rms_norm(y_a, norm_a_out[i], SGU_HEADS),
                             _head_rms_norm(y_b, norm_b_out[i], SB_HEADS)], axis=-1)
        h = h + jnp.einsum('bsd,de->bse', y, w_out[i])
        m = _rms_norm(h, norm_ffn[i])
        hid = jnp.square(jax.nn.relu(jnp.einsum('bsd,df->bsf', m, w_up[i])))
        h = h + jnp.einsum('bsf,fd->bsd', hid, w_down[i])
        gate = jax.nn.sigmoid(jnp.einsum('bsd,de->bse', _rms_norm(h, norm_ple[i]), w_ple_gate[i]))
        e = jnp.einsum('bsk,kd->bsd', p[i], w_ple_proj[i])
        h = h + gate * e
    return _rms_norm(h, norm_final)
```

```python
import functools
import math

import jax
import jax.numpy as jnp
from jax import lax
from jax.experimental import pallas as pl
from jax.experimental.pallas import tpu as pltpu

F32 = jnp.float32
BF16 = jnp.bfloat16

EPS = 1e-6
HEAD_DIM = 128
CHUNK = 128
PLE_DIM = 256

ROW_TILE = 1024
COL_TILE = 1024
PLE_COL_TILE = 512
DOWN_K_TILE = 2048
NORM_ROWS = 256
SGU_ROWS = 1024
ATTN_BLOCK = 256
VMEM_LIMIT = 56 * 1024 * 1024


def _params(*sem):
    return pltpu.CompilerParams(dimension_semantics=sem, vmem_limit_bytes=VMEM_LIMIT)


def _rmsnorm_kernel(x_ref, g_ref, o_ref):
    x = x_ref[...].astype(F32)
    inv = lax.rsqrt(jnp.mean(x * x, axis=-1, keepdims=True) + EPS)
    o_ref[...] = (x * inv * g_ref[...]).astype(o_ref.dtype)


def _rmsnorm(x, g, out_dtype):
    m, d = x.shape
    return pl.pallas_call(
        _rmsnorm_kernel,
        grid=(m // NORM_ROWS,),
        in_specs=[pl.BlockSpec((NORM_ROWS, d), lambda i: (i, 0)),
                  pl.BlockSpec((1, d), lambda i: (0, 0))],
        out_specs=pl.BlockSpec((NORM_ROWS, d), lambda i: (i, 0)),
        out_shape=jax.ShapeDtypeStruct((m, d), out_dtype),
        compiler_params=_params("parallel"),
        name="rmsnorm",
    )(x, g.reshape(1, d).astype(F32))


def _in_proj_kernel(a_ref, w_ref, o_ref, *, q_lo, q_hi, scale):
    j = pl.program_id(1)
    acc = jnp.dot(a_ref[...], w_ref[...], preferred_element_type=F32)
    s = jnp.where(jnp.logical_and(j >= q_lo, j < q_hi), scale, 1.0).astype(F32)
    o_ref[...] = (acc * s).astype(o_ref.dtype)


def _in_proj(a, w, q_col_lo, q_col_hi, scale):
    m, k = a.shape
    n = w.shape[1]
    kern = functools.partial(_in_proj_kernel, q_lo=q_col_lo // COL_TILE,
                             q_hi=q_col_hi // COL_TILE, scale=scale)
    return pl.pallas_call(
        kern,
        grid=(m // ROW_TILE, n // COL_TILE),
        in_specs=[pl.BlockSpec((ROW_TILE, k), lambda i, j: (i, 0)),
                  pl.BlockSpec((k, COL_TILE), lambda i, j: (0, j))],
        out_specs=pl.BlockSpec((ROW_TILE, COL_TILE), lambda i, j: (i, j)),
        out_shape=jax.ShapeDtypeStruct((m, n), BF16),
        compiler_params=_params("parallel", "parallel"),
        name="in_proj",
    )(a, w)


def _sgu_kernel(u_ref, v_ref, w_ref, b_ref, g_ref, o_ref):
    u = jax.nn.gelu(u_ref[...].astype(F32))
    v = jax.nn.gelu(v_ref[...].astype(F32))
    mu = jnp.mean(v, axis=-1, keepdims=True)
    vc = v - mu
    var = jnp.mean(vc * vc, axis=-1, keepdims=True)
    vn = (vc * lax.rsqrt(var + EPS)).astype(BF16)
    row = lax.broadcasted_iota(jnp.int32, (CHUNK, CHUNK), 0)
    col = lax.broadcasted_iota(jnp.int32, (CHUNK, CHUNK), 1)
    w = jnp.where(row >= col, w_ref[0], 0.0).astype(BF16)
    bias = b_ref[0]
    g = g_ref[0]
    for c in range(u.shape[0] // CHUNK):
        sl = slice(c * CHUNK, (c + 1) * CHUNK)
        mixed = jnp.dot(w, vn[sl], preferred_element_type=F32) + bias
        y = u[sl] * mixed
        inv = lax.rsqrt(jnp.mean(y * y, axis=-1, keepdims=True) + EPS)
        o_ref[sl, :] = (y * inv * g).astype(o_ref.dtype)


def _sgu(z, w_s, b_s, gain, n_heads, u_col0, v_col0):
    m = z.shape[0]
    ub, vb = u_col0 // HEAD_DIM, v_col0 // HEAD_DIM
    bias = jnp.broadcast_to(b_s.astype(F32)[:, :, None], (n_heads, CHUNK, HEAD_DIM))
    return pl.pallas_call(
        _sgu_kernel,
        grid=(m // SGU_ROWS, n_heads),
        in_specs=[pl.BlockSpec((SGU_ROWS, HEAD_DIM), lambda i, h: (i, ub + h)),
                  pl.BlockSpec((SGU_ROWS, HEAD_DIM), lambda i, h: (i, vb + h)),
                  pl.BlockSpec((1, CHUNK, CHUNK), lambda i, h: (h, 0, 0)),
                  pl.BlockSpec((1, CHUNK, HEAD_DIM), lambda i, h: (h, 0, 0)),
                  pl.BlockSpec((1, 1, HEAD_DIM), lambda i, h: (h, 0, 0))],
        out_specs=pl.BlockSpec((SGU_ROWS, HEAD_DIM), lambda i, h: (i, h)),
        out_shape=jax.ShapeDtypeStruct((m, n_heads * HEAD_DIM), BF16),
        compiler_params=_params("parallel", "parallel"),
        name="sgu",
    )(z, z, w_s.astype(F32), bias, gain.reshape(n_heads, 1, HEAD_DIM).astype(F32))


def _attn_kernel(q_ref, k_ref, v_ref, g_ref, o_ref):
    blk = ATTN_BLOCK
    qi = pl.program_id(2)
    q = q_ref[0]
    row = lax.broadcasted_iota(jnp.int32, (blk, blk), 0)
    col = lax.broadcasted_iota(jnp.int32, (blk, blk), 1)
    after = jnp.where(row > col, 1.0, 0.0).astype(BF16)
    causal = col < row

    def tile(kb, acc, carry, masked):
        k = k_ref[0, pl.ds(kb * blk, blk), :]
        v = v_ref[0, pl.ds(kb * blk, blk), :]
        z = lax.dot_general(q, k, (((1,), (1,)), ((), ())), preferred_element_type=F32)
        t = jnp.log1p(jnp.exp(-jnp.abs(z)))
        log_beta = jnp.minimum(z, 0.0) - t
        log_keep = -jnp.maximum(z, 0.0) - t
        if masked:
            log_keep = jnp.where(causal, log_keep, 0.0)
        hi = log_keep.astype(BF16)
        lo = (log_keep - hi.astype(F32)).astype(BF16)
        log_after = (jnp.dot(hi, after, preferred_element_type=F32)
                     + jnp.dot(lo, after, preferred_element_type=F32) + carry)
        a = jnp.exp(log_beta + log_after)
        if masked:
            a = jnp.where(causal, a, 0.0)
        acc = acc + jnp.dot(a.astype(BF16), v, preferred_element_type=F32)
        carry = carry + jnp.sum(log_keep, axis=-1, keepdims=True)
        return acc, carry

    acc0 = jnp.zeros((blk, HEAD_DIM), F32)
    carry0 = jnp.zeros((blk, 1), F32)
    acc, carry = tile(qi, acc0, carry0, True)

    def body(i, state):
        return tile(qi - 1 - i, state[0], state[1], False)

    acc, _ = lax.fori_loop(0, qi, body, (acc, carry))
    inv = lax.rsqrt(jnp.mean(acc * acc, axis=-1, keepdims=True) + EPS)
    o_ref[0] = (acc * inv * g_ref[0]).astype(o_ref.dtype)


def _attention(z, gain, n_heads, q_col0, k_col0, v_col0):
    b, s, _ = z.shape
    qb, kb, vb = q_col0 // HEAD_DIM, k_col0 // HEAD_DIM, v_col0 // HEAD_DIM
    return pl.pallas_call(
        _attn_kernel,
        grid=(b, n_heads, s // ATTN_BLOCK),
        in_specs=[pl.BlockSpec((1, ATTN_BLOCK, HEAD_DIM), lambda bi, h, qi: (bi, qi, qb + h)),
                  pl.BlockSpec((1, s, HEAD_DIM), lambda bi, h, qi: (bi, 0, kb + h)),
                  pl.BlockSpec((1, s, HEAD_DIM), lambda bi, h, qi: (bi, 0, vb + h)),
                  pl.BlockSpec((1, 1, HEAD_DIM), lambda bi, h, qi: (h, 0, 0))],
        out_specs=pl.BlockSpec((1, ATTN_BLOCK, HEAD_DIM), lambda bi, h, qi: (bi, qi, h)),
        out_shape=jax.ShapeDtypeStruct((b, s, n_heads * HEAD_DIM), BF16),
        compiler_params=_params("parallel", "parallel", "arbitrary"),
        name="stick_breaking",
    )(z, z, z, gain.reshape(n_heads, 1, HEAD_DIM).astype(F32))


def _out_proj_kernel(ya_ref, yb_ref, w_ref, x_ref, o_ref):
    ka = ya_ref.shape[1]
    acc = jnp.dot(ya_ref[...], w_ref[:ka, :], preferred_element_type=F32)
    acc = acc + jnp.dot(yb_ref[...], w_ref[ka:, :], preferred_element_type=F32)
    o_ref[...] = x_ref[...] + acc


def _out_proj(ya, yb, w, x):
    m, ka = ya.shape
    kb = yb.shape[1]
    n = w.shape[1]
    return pl.pallas_call(
        _out_proj_kernel,
        grid=(m // ROW_TILE, n // COL_TILE),
        in_specs=[pl.BlockSpec((ROW_TILE, ka), lambda i, j: (i, 0)),
                  pl.BlockSpec((ROW_TILE, kb), lambda i, j: (i, 0)),
                  pl.BlockSpec((ka + kb, COL_TILE), lambda i, j: (0, j)),
                  pl.BlockSpec((ROW_TILE, COL_TILE), lambda i, j: (i, j))],
        out_specs=pl.BlockSpec((ROW_TILE, COL_TILE), lambda i, j: (i, j)),
        out_shape=jax.ShapeDtypeStruct((m, n), F32),
        compiler_params=_params("parallel", "parallel"),
        name="out_proj",
    )(ya, yb, w, x)


def _mlp_up_kernel(a_ref, w_ref, o_ref):
    acc = jnp.dot(a_ref[...], w_ref[...], preferred_element_type=F32)
    o_ref[...] = jnp.square(jnp.maximum(acc, 0.0)).astype(o_ref.dtype)


def _mlp_up(a, w):
    m, k = a.shape
    n = w.shape[1]
    return pl.pallas_call(
        _mlp_up_kernel,
        grid=(m // ROW_TILE, n // COL_TILE),
        in_specs=[pl.BlockSpec((ROW_TILE, k), lambda i, j: (i, 0)),
                  pl.BlockSpec((k, COL_TILE), lambda i, j: (0, j))],
        out_specs=pl.BlockSpec((ROW_TILE, COL_TILE), lambda i, j: (i, j)),
        out_shape=jax.ShapeDtypeStruct((m, n), BF16),
        compiler_params=_params("parallel", "parallel"),
        name="mlp_up",
    )(a, w)


def _mlp_down_kernel(a_ref, w_ref, r_ref, o_ref, acc_ref):
    kk = pl.program_id(2)

    @pl.when(kk == 0)
    def _():
        acc_ref[...] = jnp.zeros_like(acc_ref)

    acc_ref[...] += jnp.dot(a_ref[...], w_ref[...], preferred_element_type=F32)

    @pl.when(kk == pl.num_programs(2) - 1)
    def _():
        o_ref[...] = r_ref[...] + acc_ref[...]


def _mlp_down(a, w, r):
    m, k = a.shape
    n = w.shape[1]
    return pl.pallas_call(
        _mlp_down_kernel,
        grid=(m // ROW_TILE, n // COL_TILE, k // DOWN_K_TILE),
        in_specs=[pl.BlockSpec((ROW_TILE, DOWN_K_TILE), lambda i, j, kk: (i, kk)),
                  pl.BlockSpec((DOWN_K_TILE, COL_TILE), lambda i, j, kk: (kk, j)),
                  pl.BlockSpec((ROW_TILE, COL_TILE), lambda i, j, kk: (i, j))],
        out_specs=pl.BlockSpec((ROW_TILE, COL_TILE), lambda i, j, kk: (i, j)),
        out_shape=jax.ShapeDtypeStruct((m, n), F32),
        scratch_shapes=[pltpu.VMEM((ROW_TILE, COL_TILE), F32)],
        compiler_params=_params("parallel", "parallel", "arbitrary"),
        name="mlp_down",
    )(a, w, r)


def _ple_kernel(a_ref, wg_ref, p_ref, wp_ref, h_ref, o_ref):
    gate = jax.nn.sigmoid(jnp.dot(a_ref[...], wg_ref[...], preferred_element_type=F32))
    e = jnp.dot(p_ref[...].astype(BF16), wp_ref[...], preferred_element_type=F32)
    o_ref[...] = h_ref[...] + gate * e


def _ple(a, w_gate, p, w_proj, h):
    m, k = a.shape
    n = w_gate.shape[1]
    kp = p.shape[1]
    return pl.pallas_call(
        _ple_kernel,
        grid=(m // ROW_TILE, n // PLE_COL_TILE),
        in_specs=[pl.BlockSpec((ROW_TILE, k), lambda i, j: (i, 0)),
                  pl.BlockSpec((k, PLE_COL_TILE), lambda i, j: (0, j)),
                  pl.BlockSpec((ROW_TILE, kp), lambda i, j: (i, 0)),
                  pl.BlockSpec((kp, PLE_COL_TILE), lambda i, j: (0, j)),
                  pl.BlockSpec((ROW_TILE, PLE_COL_TILE), lambda i, j: (i, j))],
        out_specs=pl.BlockSpec((ROW_TILE, PLE_COL_TILE), lambda i, j: (i, j)),
        out_shape=jax.ShapeDtypeStruct((m, n), F32),
        compiler_params=_params("parallel", "parallel"),
        name="ple_gate",
    )(a, w_gate, p, w_proj, h)


def kernel(x, p, norm_mix, w_in, w_s, b_s, norm_a_out, norm_b_out, w_out, norm_ffn, w_up,
           w_down, norm_ple, w_ple_gate, w_ple_proj, norm_final):
    batch, seq, d_model = x.shape
    depth = w_in.shape[0]
    a_width = norm_a_out.shape[1]
    b_width = norm_b_out.shape[1]
    sgu_heads = a_width // HEAD_DIM
    sb_heads = b_width // HEAD_DIM
    m = batch * seq
    scale = 1.0 / math.sqrt(HEAD_DIM)
    o1, o2 = a_width, 2 * a_width
    o3, o4 = o2 + b_width, o2 + 2 * b_width

    h = x.reshape(m, d_model)
    for i in range(depth):
        a = _rmsnorm(h, norm_mix[i], BF16)
        z = _in_proj(a, w_in[i].astype(BF16), o2, o3, scale)
        y_a = _sgu(z, w_s[i], b_s[i], norm_a_out[i], sgu_heads, 0, o1)
        y_b = _attention(z.reshape(batch, seq, -1), norm_b_out[i], sb_heads, o2, o3, o4)
        h = _out_proj(y_a, y_b.reshape(m, b_width), w_out[i].astype(BF16), h)
        mlp_in = _rmsnorm(h, norm_ffn[i], BF16)
        hid = _mlp_up(mlp_in, w_up[i].astype(BF16))
        h = _mlp_down(hid, w_down[i].astype(BF16), h)
        gate_in = _rmsnorm(h, norm_ple[i], BF16)
        h = _ple(gate_in, w_ple_gate[i].astype(BF16), p[i].reshape(m, PLE_DIM),
                 w_ple_proj[i].astype(BF16), h)
    out = _rmsnorm(h, norm_final, x.dtype)
    return out.reshape(batch, seq, d_model)
```

```python
import functools
import math

import jax
import jax.numpy as jnp
from jax import lax
from jax.experimental import pallas as pl
from jax.experimental.pallas import tpu as pltpu

F32 = jnp.float32
BF16 = jnp.bfloat16

EPS = 1e-6
HEAD_DIM = 128
CHUNK = 128
PLE_DIM = 256

ROW_TILE = 1024
COL_TILE = 1024
OUT_COL_TILE = 512
PLE_ROW_TILE = 512
DOWN_K_TILE = 2048
NORM_ROWS = 256
SGU_ROWS = 1024
ATTN_BLOCK = 256
ATTN_HEADS = 2
ATTN_SKIP_LOG = -110.0
VMEM_LIMIT = 56 * 1024 * 1024


def _params(*sem):
    return pltpu.CompilerParams(dimension_semantics=sem, vmem_limit_bytes=VMEM_LIMIT)


def _rmsnorm_kernel(x_ref, g_ref, o_ref):
    x = x_ref[...].astype(F32)
    inv = lax.rsqrt(jnp.mean(x * x, axis=-1, keepdims=True) + EPS)
    o_ref[...] = (x * inv * g_ref[...]).astype(o_ref.dtype)


def _rmsnorm(x, g, out_dtype):
    m, d = x.shape
    return pl.pallas_call(
        _rmsnorm_kernel,
        grid=(m // NORM_ROWS,),
        in_specs=[pl.BlockSpec((NORM_ROWS, d), lambda i: (i, 0)),
                  pl.BlockSpec((1, d), lambda i: (0, 0))],
        out_specs=pl.BlockSpec((NORM_ROWS, d), lambda i: (i, 0)),
        out_shape=jax.ShapeDtypeStruct((m, d), out_dtype),
        compiler_params=_params("parallel"),
        name="rmsnorm",
    )(x, g.reshape(1, d).astype(F32))


def _in_proj_kernel(a_ref, w_ref, o_ref, *, q_lo, q_hi, scale):
    j = pl.program_id(1)
    acc = jnp.dot(a_ref[...], w_ref[...], preferred_element_type=F32)
    s = jnp.where(jnp.logical_and(j >= q_lo, j < q_hi), scale, 1.0).astype(F32)
    o_ref[...] = (acc * s).astype(o_ref.dtype)


def _in_proj(a, w, q_col_lo, q_col_hi, scale):
    m, k = a.shape
    n = w.shape[1]
    kern = functools.partial(_in_proj_kernel, q_lo=q_col_lo // COL_TILE,
                             q_hi=q_col_hi // COL_TILE, scale=scale)
    return pl.pallas_call(
        kern,
        grid=(m // ROW_TILE, n // COL_TILE),
        in_specs=[pl.BlockSpec((ROW_TILE, k), lambda i, j: (i, 0)),
                  pl.BlockSpec((k, COL_TILE), lambda i, j: (0, j))],
        out_specs=pl.BlockSpec((ROW_TILE, COL_TILE), lambda i, j: (i, j)),
        out_shape=jax.ShapeDtypeStruct((m, n), BF16),
        compiler_params=_params("parallel", "parallel"),
        name="in_proj",
    )(a, w)


def _sgu_kernel(u_ref, v_ref, w_ref, b_ref, g_ref, o_ref):
    u = jax.nn.gelu(u_ref[...].astype(F32))
    v = jax.nn.gelu(v_ref[...].astype(F32))
    mu = jnp.mean(v, axis=-1, keepdims=True)
    vc = v - mu
    var = jnp.mean(vc * vc, axis=-1, keepdims=True)
    vn = (vc * lax.rsqrt(var + EPS)).astype(BF16)
    row = lax.broadcasted_iota(jnp.int32, (CHUNK, CHUNK), 0)
    col = lax.broadcasted_iota(jnp.int32, (CHUNK, CHUNK), 1)
    w = jnp.where(row >= col, w_ref[0], 0.0).astype(BF16)
    bias = b_ref[0]
    g = g_ref[0]
    for c in range(u.shape[0] // CHUNK):
        sl = slice(c * CHUNK, (c + 1) * CHUNK)
        mixed = jnp.dot(w, vn[sl], preferred_element_type=F32) + bias
        y = u[sl] * mixed
        inv = lax.rsqrt(jnp.mean(y * y, axis=-1, keepdims=True) + EPS)
        o_ref[sl, :] = (y * inv * g).astype(o_ref.dtype)


def _sgu(z, w_s, b_s, gain, n_heads, u_col0, v_col0):
    m = z.shape[0]
    ub, vb = u_col0 // HEAD_DIM, v_col0 // HEAD_DIM
    bias = jnp.broadcast_to(b_s.astype(F32)[:, :, None], (n_heads, CHUNK, HEAD_DIM))
    return pl.pallas_call(
        _sgu_kernel,
        grid=(m // SGU_ROWS, n_heads),
        in_specs=[pl.BlockSpec((SGU_ROWS, HEAD_DIM), lambda i, h: (i, ub + h)),
                  pl.BlockSpec((SGU_ROWS, HEAD_DIM), lambda i, h: (i, vb + h)),
                  pl.BlockSpec((1, CHUNK, CHUNK), lambda i, h: (h, 0, 0)),
                  pl.BlockSpec((1, CHUNK, HEAD_DIM), lambda i, h: (h, 0, 0)),
                  pl.BlockSpec((1, 1, HEAD_DIM), lambda i, h: (h, 0, 0))],
        out_specs=pl.BlockSpec((SGU_ROWS, HEAD_DIM), lambda i, h: (i, h)),
        out_shape=jax.ShapeDtypeStruct((m, n_heads * HEAD_DIM), BF16),
        compiler_params=_params("parallel", "parallel"),
        name="sgu",
    )(z, z, w_s.astype(F32), bias, gain.reshape(n_heads, 1, HEAD_DIM).astype(F32))


def _attn_kernel(q_ref, k_ref, v_ref, g_ref, o_ref):
    blk = ATTN_BLOCK
    n_h = q_ref.shape[2] // HEAD_DIM
    qi = pl.program_id(2)
    row = lax.broadcasted_iota(jnp.int32, (blk, blk), 0)
    col = lax.broadcasted_iota(jnp.int32, (blk, blk), 1)
    after = jnp.where(row > col, 1.0, 0.0).astype(BF16)
    after2 = jnp.concatenate([after, after], axis=0)
    causal = col < row

    def tile(h, kb, carry, masked):
        lanes = slice(h * HEAD_DIM, (h + 1) * HEAD_DIM)
        q = q_ref[0, :, lanes]
        k = k_ref[0, pl.ds(kb * blk, blk), lanes]
        v = v_ref[0, pl.ds(kb * blk, blk), lanes]
        z = lax.dot_general(q, k, (((1,), (1,)), ((), ())), preferred_element_type=F32)
        t = jnp.log(1.0 + jnp.exp(-jnp.abs(z)))
        log_beta = jnp.minimum(z, 0.0) - t
        log_keep = log_beta - z
        if masked:
            log_keep = jnp.where(causal, log_keep, 0.0)
        hi = log_keep.astype(BF16)
        lo = (log_keep - hi.astype(F32)).astype(BF16)
        log_after = jnp.dot(jnp.concatenate([hi, lo], axis=1), after2,
                            preferred_element_type=F32) + carry
        a = jnp.exp(log_beta + log_after)
        if masked:
            a = jnp.where(causal, a, 0.0)
        contrib = jnp.dot(a.astype(BF16), v, preferred_element_type=F32)
        return contrib, carry + jnp.sum(log_keep, axis=-1, keepdims=True)

    def live(carries):
        top = carries[0]
        for c in carries[1:]:
            top = jnp.maximum(top, c)
        return (jnp.max(top) > ATTN_SKIP_LOG).astype(jnp.int32)

    zero_carry = jnp.zeros((blk, 1), F32)
    first = [tile(h, qi, zero_carry, True) for h in range(n_h)]
    accs = tuple(f[0] for f in first)
    carries = tuple(f[1] for f in first)

    def cond(state):
        return jnp.logical_and(state[0] >= 0, state[1] > 0)

    def body(state):
        kb, _, accs, carries = state
        out = [tile(h, kb, carries[h], False) for h in range(n_h)]
        accs = tuple(a + o[0] for a, o in zip(accs, out))
        carries = tuple(o[1] for o in out)
        return kb - 1, live(carries), accs, carries

    _, _, accs, _ = lax.while_loop(cond, body, (qi - 1, live(carries), accs, carries))
    for h in range(n_h):
        lanes = slice(h * HEAD_DIM, (h + 1) * HEAD_DIM)
        acc = accs[h]
        inv = lax.rsqrt(jnp.mean(acc * acc, axis=-1, keepdims=True) + EPS)
        o_ref[0, :, lanes] = (acc * inv * g_ref[0, :, lanes]).astype(o_ref.dtype)


def _attention(z, gain, n_heads, q_col0, k_col0, v_col0):
    b, s, _ = z.shape
    width = ATTN_HEADS * HEAD_DIM
    qb, kb, vb = q_col0 // width, k_col0 // width, v_col0 // width
    return pl.pallas_call(
        _attn_kernel,
        grid=(b, n_heads // ATTN_HEADS, s // ATTN_BLOCK),
        in_specs=[pl.BlockSpec((1, ATTN_BLOCK, width), lambda bi, h, qi: (bi, qi, qb + h)),
                  pl.BlockSpec((1, s, width), lambda bi, h, qi: (bi, 0, kb + h)),
                  pl.BlockSpec((1, s, width), lambda bi, h, qi: (bi, 0, vb + h)),
                  pl.BlockSpec((1, 1, width), lambda bi, h, qi: (0, 0, h))],
        out_specs=pl.BlockSpec((1, ATTN_BLOCK, width), lambda bi, h, qi: (bi, qi, h)),
        out_shape=jax.ShapeDtypeStruct((b, s, n_heads * HEAD_DIM), BF16),
        compiler_params=_params("parallel", "parallel", "arbitrary"),
        name="stick_breaking",
    )(z, z, z, gain.reshape(1, 1, n_heads * HEAD_DIM).astype(F32))


def _store_residual(h, first_col_tile, gain_ref, h_ref, hg_ref, ssq_ref):
    h_ref[...] = h
    hg_ref[...] = (h * gain_ref[...]).astype(hg_ref.dtype)

    @pl.when(first_col_tile)
    def _():
        ssq_ref[...] = jnp.zeros_like(ssq_ref)

    ssq_ref[...] += jnp.sum(h * h, axis=-1, keepdims=True)


def _row_inv_rms(ssq_ref, width):
    return lax.rsqrt(ssq_ref[...] * (1.0 / width) + EPS)


def _residual_out(m, n):
    return (jax.ShapeDtypeStruct((m, n), F32), jax.ShapeDtypeStruct((m, n), BF16),
            jax.ShapeDtypeStruct((m, 1), F32))


def _out_proj_kernel(ya_ref, yb_ref, w_ref, x_ref, gain_ref, h_ref, hg_ref, ssq_ref):
    ka = ya_ref.shape[1]
    acc = jnp.dot(ya_ref[...], w_ref[:ka, :], preferred_element_type=F32)
    acc = acc + jnp.dot(yb_ref[...], w_ref[ka:, :], preferred_element_type=F32)
    _store_residual(x_ref[...] + acc, pl.program_id(1) == 0, gain_ref, h_ref, hg_ref, ssq_ref)


def _out_proj(ya, yb, w, x, next_gain):
    m, ka = ya.shape
    kb = yb.shape[1]
    n = w.shape[1]
    tile = pl.BlockSpec((ROW_TILE, OUT_COL_TILE), lambda i, j: (i, j))
    return pl.pallas_call(
        _out_proj_kernel,
        grid=(m // ROW_TILE, n // OUT_COL_TILE),
        in_specs=[pl.BlockSpec((ROW_TILE, ka), lambda i, j: (i, 0)),
                  pl.BlockSpec((ROW_TILE, kb), lambda i, j: (i, 0)),
                  pl.BlockSpec((ka + kb, OUT_COL_TILE), lambda i, j: (0, j)),
                  tile,
                  pl.BlockSpec((1, OUT_COL_TILE), lambda i, j: (0, j))],
        out_specs=(tile, tile, pl.BlockSpec((ROW_TILE, 1), lambda i, j: (i, 0))),
        out_shape=_residual_out(m, n),
        compiler_params=_params("parallel", "arbitrary"),
        name="out_proj",
    )(ya, yb, w, x, next_gain.reshape(1, n).astype(F32))


def _mlp_up_kernel(a_ref, w_ref, ssq_ref, o_ref):
    acc = jnp.dot(a_ref[...], w_ref[...], preferred_element_type=F32)
    pre = acc * _row_inv_rms(ssq_ref, a_ref.shape[1])
    o_ref[...] = jnp.square(jnp.maximum(pre, 0.0)).astype(o_ref.dtype)


def _mlp_up(a, ssq, w):
    m, k = a.shape
    n = w.shape[1]
    return pl.pallas_call(
        _mlp_up_kernel,
        grid=(m // ROW_TILE, n // COL_TILE),
        in_specs=[pl.BlockSpec((ROW_TILE, k), lambda i, j: (i, 0)),
                  pl.BlockSpec((k, COL_TILE), lambda i, j: (0, j)),
                  pl.BlockSpec((ROW_TILE, 1), lambda i, j: (i, 0))],
        out_specs=pl.BlockSpec((ROW_TILE, COL_TILE), lambda i, j: (i, j)),
        out_shape=jax.ShapeDtypeStruct((m, n), BF16),
        compiler_params=_params("parallel", "parallel"),
        name="mlp_up",
    )(a, w, ssq)


def _mlp_down_kernel(a_ref, w_ref, r_ref, gain_ref, h_ref, hg_ref, ssq_ref, acc_ref):
    kk = pl.program_id(2)

    @pl.when(kk == 0)
    def _():
        acc_ref[...] = jnp.zeros_like(acc_ref)

    acc_ref[...] += jnp.dot(a_ref[...], w_ref[...], preferred_element_type=F32)

    @pl.when(kk == pl.num_programs(2) - 1)
    def _():
        _store_residual(r_ref[...] + acc_ref[...], pl.program_id(1) == 0, gain_ref,
                        h_ref, hg_ref, ssq_ref)


def _mlp_down(a, w, r, next_gain):
    m, k = a.shape
    n = w.shape[1]
    tile = pl.BlockSpec((ROW_TILE, COL_TILE), lambda i, j, kk: (i, j))
    return pl.pallas_call(
        _mlp_down_kernel,
        grid=(m // ROW_TILE, n // COL_TILE, k // DOWN_K_TILE),
        in_specs=[pl.BlockSpec((ROW_TILE, DOWN_K_TILE), lambda i, j, kk: (i, kk)),
                  pl.BlockSpec((DOWN_K_TILE, COL_TILE), lambda i, j, kk: (kk, j)),
                  tile,
                  pl.BlockSpec((1, COL_TILE), lambda i, j, kk: (0, j))],
        out_specs=(tile, tile, pl.BlockSpec((ROW_TILE, 1), lambda i, j, kk: (i, 0))),
        out_shape=_residual_out(m, n),
        scratch_shapes=[pltpu.VMEM((ROW_TILE, COL_TILE), F32)],
        compiler_params=_params("parallel", "arbitrary", "arbitrary"),
        name="mlp_down",
    )(a, w, r, next_gain.reshape(1, n).astype(F32))


def _ple_kernel(a_ref, wg_ref, ssq_ref, p_ref, wp_ref, h_ref, gf_ref, o_ref, ssq_out_ref):
    j = pl.program_id(1)
    logits = jnp.dot(a_ref[...], wg_ref[...], preferred_element_type=F32)
    gate = jax.nn.sigmoid(logits * _row_inv_rms(ssq_ref, a_ref.shape[1]))
    e = jnp.dot(p_ref[...].astype(BF16), wp_ref[...], preferred_element_type=F32)
    h = h_ref[...] + gate * e
    bn = h.shape[1]
    for jj in range(o_ref.shape[1] // bn):
        @pl.when(j == jj)
        def _(jj=jj):
            o_ref[:, jj * bn:(jj + 1) * bn] = h

    @pl.when(j == 0)
    def _():
        ssq_out_ref[...] = jnp.zeros_like(ssq_out_ref)

    ssq_out_ref[...] += jnp.sum(h * h, axis=-1, keepdims=True)

    @pl.when(j == pl.num_programs(1) - 1)
    def _():
        inv = _row_inv_rms(ssq_out_ref, o_ref.shape[1])
        for jj in range(o_ref.shape[1] // bn):
            cols = slice(jj * bn, (jj + 1) * bn)
            o_ref[:, cols] = o_ref[:, cols] * inv * gf_ref[:, cols]


def _ple_final(a, ssq, w_gate, p, w_proj, h, final_gain):
    m, k = a.shape
    n = w_gate.shape[1]
    kp = p.shape[1]
    return pl.pallas_call(
        _ple_kernel,
        grid=(m // PLE_ROW_TILE, n // COL_TILE),
        in_specs=[pl.BlockSpec((PLE_ROW_TILE, k), lambda i, j: (i, 0)),
                  pl.BlockSpec((k, COL_TILE), lambda i, j: (0, j)),
                  pl.BlockSpec((PLE_ROW_TILE, 1), lambda i, j: (i, 0)),
                  pl.BlockSpec((PLE_ROW_TILE, kp), lambda i, j: (i, 0)),
                  pl.BlockSpec((kp, COL_TILE), lambda i, j: (0, j)),
                  pl.BlockSpec((PLE_ROW_TILE, COL_TILE), lambda i, j: (i, j)),
                  pl.BlockSpec((1, n), lambda i, j: (0, 0))],
        out_specs=pl.BlockSpec((PLE_ROW_TILE, n), lambda i, j: (i, 0)),
        out_shape=jax.ShapeDtypeStruct((m, n), F32),
        scratch_shapes=[pltpu.VMEM((PLE_ROW_TILE, 1), F32)],
        compiler_params=_params("parallel", "arbitrary"),
        name="ple_gate",
    )(a, w_gate, ssq, p, w_proj, h, final_gain.reshape(1, n).astype(F32))


def kernel(x, p, norm_mix, w_in, w_s, b_s, norm_a_out, norm_b_out, w_out, norm_ffn, w_up,
           w_down, norm_ple, w_ple_gate, w_ple_proj, norm_final):
    batch, seq, d_model = x.shape
    assert w_in.shape[0] == 1, "single layer: the gated-embedding kernel applies the final rmsnorm"
    a_width = norm_a_out.shape[1]
    b_width = norm_b_out.shape[1]
    sgu_heads = a_width // HEAD_DIM
    sb_heads = b_width // HEAD_DIM
    m = batch * seq
    scale = 1.0 / math.sqrt(HEAD_DIM)
    o1, o2 = a_width, 2 * a_width
    o3, o4 = o2 + b_width, o2 + 2 * b_width

    h = x.reshape(m, d_model)
    a = _rmsnorm(h, norm_mix[0], BF16)
    z = _in_proj(a, w_in[0].astype(BF16), o2, o3, scale)
    y_a = _sgu(z, w_s[0], b_s[0], norm_a_out[0], sgu_heads, 0, o1)
    y_b = _attention(z.reshape(batch, seq, -1), norm_b_out[0], sb_heads, o2, o3, o4)
    h, hg, ssq = _out_proj(y_a, y_b.reshape(m, b_width), w_out[0].astype(BF16), h, norm_ffn[0])
    hid = _mlp_up(hg, ssq, w_up[0].astype(BF16))
    h, hg, ssq = _mlp_down(hid, w_down[0].astype(BF16), h, norm_ple[0])
    out = _ple_final(hg, ssq, w_ple_gate[0].astype(BF16), p[0].reshape(m, PLE_DIM),
                     w_ple_proj[0].astype(BF16), h, norm_final)
    return out.reshape(batch, seq, d_model).astype(x.dtype)
```

```python
import functools
import math

import jax
import jax.numpy as jnp
from jax import lax
from jax.experimental import pallas as pl
from jax.experimental.pallas import tpu as pltpu

F32 = jnp.float32
BF16 = jnp.bfloat16

EPS = 1e-6
HEAD_DIM = 128
CHUNK = 128
PLE_DIM = 256

ROW_TILE = 1024
COL_TILE = 1024
OUT_COL_TILE = 512
PLE_ROW_TILE = 512
DOWN_K_TILE = 2048
NORM_ROWS = 256
SGU_ROWS = 1024
ATTN_BLOCK = 256
ATTN_HEADS = 4
ATTN_SKIP_LOG = -110.0
VMEM_LIMIT = 56 * 1024 * 1024


def _params(*sem):
    return pltpu.CompilerParams(dimension_semantics=sem, vmem_limit_bytes=VMEM_LIMIT)


def _rmsnorm_kernel(x_ref, g_ref, o_ref):
    x = x_ref[...].astype(F32)
    inv = lax.rsqrt(jnp.mean(x * x, axis=-1, keepdims=True) + EPS)
    o_ref[...] = (x * inv * g_ref[...]).astype(o_ref.dtype)


def _rmsnorm(x, g, out_dtype):
    m, d = x.shape
    return pl.pallas_call(
        _rmsnorm_kernel,
        grid=(m // NORM_ROWS,),
        in_specs=[pl.BlockSpec((NORM_ROWS, d), lambda i: (i, 0)),
                  pl.BlockSpec((1, d), lambda i: (0, 0))],
        out_specs=pl.BlockSpec((NORM_ROWS, d), lambda i: (i, 0)),
        out_shape=jax.ShapeDtypeStruct((m, d), out_dtype),
        compiler_params=_params("parallel"),
        name="rmsnorm",
    )(x, g.reshape(1, d).astype(F32))


def _in_proj_kernel(a_ref, w_ref, o_ref, *, q_lo, q_hi, scale):
    j = pl.program_id(1)
    acc = jnp.dot(a_ref[...], w_ref[...], preferred_element_type=F32)
    s = jnp.where(jnp.logical_and(j >= q_lo, j < q_hi), scale, 1.0).astype(F32)
    o_ref[...] = (acc * s).astype(o_ref.dtype)


def _in_proj(a, w, q_col_lo, q_col_hi, scale):
    m, k = a.shape
    n = w.shape[1]
    kern = functools.partial(_in_proj_kernel, q_lo=q_col_lo // COL_TILE,
                             q_hi=q_col_hi // COL_TILE, scale=scale)
    return pl.pallas_call(
        kern,
        grid=(m // ROW_TILE, n // COL_TILE),
        in_specs=[pl.BlockSpec((ROW_TILE, k), lambda i, j: (i, 0)),
                  pl.BlockSpec((k, COL_TILE), lambda i, j: (0, j))],
        out_specs=pl.BlockSpec((ROW_TILE, COL_TILE), lambda i, j: (i, j)),
        out_shape=jax.ShapeDtypeStruct((m, n), BF16),
        compiler_params=_params("parallel", "parallel"),
        name="in_proj",
    )(a, w)


def _sgu_kernel(u_ref, v_ref, w_ref, b_ref, g_ref, o_ref):
    u = jax.nn.gelu(u_ref[...].astype(F32))
    v = jax.nn.gelu(v_ref[...].astype(F32))
    mu = jnp.mean(v, axis=-1, keepdims=True)
    vc = v - mu
    var = jnp.mean(vc * vc, axis=-1, keepdims=True)
    vn = (vc * lax.rsqrt(var + EPS)).astype(BF16)
    row = lax.broadcasted_iota(jnp.int32, (CHUNK, CHUNK), 0)
    col = lax.broadcasted_iota(jnp.int32, (CHUNK, CHUNK), 1)
    w = jnp.where(row >= col, w_ref[0], 0.0).astype(BF16)
    bias = b_ref[0]
    g = g_ref[0]
    for c in range(u.shape[0] // CHUNK):
        sl = slice(c * CHUNK, (c + 1) * CHUNK)
        mixed = jnp.dot(w, vn[sl], preferred_element_type=F32) + bias
        y = u[sl] * mixed
        inv = lax.rsqrt(jnp.mean(y * y, axis=-1, keepdims=True) + EPS)
        o_ref[sl, :] = (y * inv * g).astype(o_ref.dtype)


def _sgu(z, w_s, b_s, gain, n_heads, u_col0, v_col0):
    m = z.shape[0]
    ub, vb = u_col0 // HEAD_DIM, v_col0 // HEAD_DIM
    bias = jnp.broadcast_to(b_s.astype(F32)[:, :, None], (n_heads, CHUNK, HEAD_DIM))
    return pl.pallas_call(
        _sgu_kernel,
        grid=(m // SGU_ROWS, n_heads),
        in_specs=[pl.BlockSpec((SGU_ROWS, HEAD_DIM), lambda i, h: (i, ub + h)),
                  pl.BlockSpec((SGU_ROWS, HEAD_DIM), lambda i, h: (i, vb + h)),
                  pl.BlockSpec((1, CHUNK, CHUNK), lambda i, h: (h, 0, 0)),
                  pl.BlockSpec((1, CHUNK, HEAD_DIM), lambda i, h: (h, 0, 0)),
                  pl.BlockSpec((1, 1, HEAD_DIM), lambda i, h: (h, 0, 0))],
        out_specs=pl.BlockSpec((SGU_ROWS, HEAD_DIM), lambda i, h: (i, h)),
        out_shape=jax.ShapeDtypeStruct((m, n_heads * HEAD_DIM), BF16),
        compiler_params=_params("parallel", "parallel"),
        name="sgu",
    )(z, z, w_s.astype(F32), bias, gain.reshape(n_heads, 1, HEAD_DIM).astype(F32))


def _attn_kernel(q_ref, k_ref, v_ref, g_ref, o_ref, acc_ref, carry_ref):
    blk = ATTN_BLOCK
    heads = range(q_ref.shape[2] // HEAD_DIM)
    qi = pl.program_id(2)
    row = lax.broadcasted_iota(jnp.int32, (blk, blk), 0)
    col = lax.broadcasted_iota(jnp.int32, (blk, blk), 1)
    after = jnp.where(row > col, 1.0, 0.0).astype(BF16)
    after2 = jnp.concatenate([after, after], axis=0)
    causal = col < row

    def lanes(h):
        return slice(h * HEAD_DIM, (h + 1) * HEAD_DIM)

    def scores(h, kb):
        k = k_ref[0, pl.ds(kb * blk, blk), lanes(h)]
        return lax.dot_general(q_ref[0, :, lanes(h)], k, (((1,), (1,)), ((), ())),
                               preferred_element_type=F32)

    def log_terms(z, masked):
        t = jnp.log(1.0 + jnp.exp(-jnp.abs(z)))
        log_beta = jnp.minimum(z, 0.0) - t
        log_keep = log_beta - z
        if masked:
            log_keep = jnp.where(causal, log_keep, 0.0)
        hi = log_keep.astype(BF16)
        lo = (log_keep - hi.astype(F32)).astype(BF16)
        return log_beta, log_keep, jnp.concatenate([hi, lo], axis=1)

    def attend(tiles, first):
        z = [[scores(h, kb) for kb, _ in tiles] for h in heads]
        logs = [[log_terms(z[h][t], tiles[t][1]) for t in range(len(tiles))] for h in heads]
        sums = [[jnp.dot(lg[2], after2, preferred_element_type=F32) for lg in logs[h]]
                for h in heads]
        weights = []
        for h in heads:
            carry = None if first else carry_ref[h]
            row_w = []
            for t, (_, masked) in enumerate(tiles):
                log_beta, log_keep, _ = logs[h][t]
                log_after = sums[h][t] if carry is None else sums[h][t] + carry
                a = jnp.exp(log_beta + log_after)
                if masked:
                    a = jnp.where(causal, a, 0.0)
                row_w.append(a.astype(BF16))
                block_sum = jnp.sum(log_keep, axis=-1, keepdims=True)
                carry = block_sum if carry is None else carry + block_sum
            carry_ref[h] = carry
            weights.append(row_w)
        for h in heads:
            out = None
            for t, (kb, _) in enumerate(tiles):
                v = v_ref[0, pl.ds(kb * blk, blk), lanes(h)]
                part = jnp.dot(weights[h][t], v, preferred_element_type=F32)
                out = part if out is None else out + part
            acc_ref[h] = out if first else acc_ref[h] + out

    def live():
        top = carry_ref[0]
        for h in heads[1:]:
            top = jnp.maximum(top, carry_ref[h])
        return (jnp.max(top) > ATTN_SKIP_LOG).astype(jnp.int32)

    @pl.when(qi == 0)
    def _():
        attend([(qi, True)], True)

    @pl.when(qi > 0)
    def _():
        attend([(qi, True), (qi - 1, False)], True)

        def cond(state):
            return jnp.logical_and(state[0] >= 0, state[1] > 0)

        def body(state):
            attend([(state[0], False)], False)
            return state[0] - 1, live()

        lax.while_loop(cond, body, (qi - 2, live()))

    for h in heads:
        acc = acc_ref[h]
        inv = lax.rsqrt(jnp.mean(acc * acc, axis=-1, keepdims=True) + EPS)
        o_ref[0, :, lanes(h)] = (acc * inv * g_ref[0, :, lanes(h)]).astype(o_ref.dtype)


def _attention(z, gain, n_heads, q_col0, k_col0, v_col0):
    b, s, _ = z.shape
    width = ATTN_HEADS * HEAD_DIM
    qb, kb, vb = q_col0 // width, k_col0 // width, v_col0 // width
    return pl.pallas_call(
        _attn_kernel,
        grid=(b, n_heads // ATTN_HEADS, s // ATTN_BLOCK),
        in_specs=[pl.BlockSpec((1, ATTN_BLOCK, width), lambda bi, h, qi: (bi, qi, qb + h)),
                  pl.BlockSpec((1, s, width), lambda bi, h, qi: (bi, 0, kb + h)),
                  pl.BlockSpec((1, s, width), lambda bi, h, qi: (bi, 0, vb + h)),
                  pl.BlockSpec((1, 1, width), lambda bi, h, qi: (0, 0, h))],
        out_specs=pl.BlockSpec((1, ATTN_BLOCK, width), lambda bi, h, qi: (bi, qi, h)),
        out_shape=jax.ShapeDtypeStruct((b, s, n_heads * HEAD_DIM), BF16),
        scratch_shapes=[pltpu.VMEM((ATTN_HEADS, ATTN_BLOCK, HEAD_DIM), F32),
                        pltpu.VMEM((ATTN_HEADS, ATTN_BLOCK, 1), F32)],
        compiler_params=_params("parallel", "parallel", "arbitrary"),
        name="stick_breaking",
    )(z, z, z, gain.reshape(1, 1, n_heads * HEAD_DIM).astype(F32))


def _store_residual(h, first_col_tile, gain_ref, h_ref, hg_ref, ssq_ref):
    h_ref[...] = h
    hg_ref[...] = (h * gain_ref[...]).astype(hg_ref.dtype)

    @pl.when(first_col_tile)
    def _():
        ssq_ref[...] = jnp.zeros_like(ssq_ref)

    ssq_ref[...] += jnp.sum(h * h, axis=-1, keepdims=True)


def _row_inv_rms(ssq_ref, width):
    return lax.rsqrt(ssq_ref[...] * (1.0 / width) + EPS)


def _residual_out(m, n):
    return (jax.ShapeDtypeStruct((m, n), F32), jax.ShapeDtypeStruct((m, n), BF16),
            jax.ShapeDtypeStruct((m, 1), F32))


def _out_proj_kernel(ya_ref, yb_ref, w_ref, x_ref, gain_ref, h_ref, hg_ref, ssq_ref):
    ka = ya_ref.shape[1]
    acc = jnp.dot(ya_ref[...], w_ref[:ka, :], preferred_element_type=F32)
    acc = acc + jnp.dot(yb_ref[...], w_ref[ka:, :], preferred_element_type=F32)
    _store_residual(x_ref[...] + acc, pl.program_id(1) == 0, gain_ref, h_ref, hg_ref, ssq_ref)


def _out_proj(ya, yb, w, x, next_gain):
    m, ka = ya.shape
    kb = yb.shape[1]
    n = w.shape[1]
    tile = pl.BlockSpec((ROW_TILE, OUT_COL_TILE), lambda i, j: (i, j))
    return pl.pallas_call(
        _out_proj_kernel,
        grid=(m // ROW_TILE, n // OUT_COL_TILE),
        in_specs=[pl.BlockSpec((ROW_TILE, ka), lambda i, j: (i, 0)),
                  pl.BlockSpec((ROW_TILE, kb), lambda i, j: (i, 0)),
                  pl.BlockSpec((ka + kb, OUT_COL_TILE), lambda i, j: (0, j)),
                  tile,
                  pl.BlockSpec((1, OUT_COL_TILE), lambda i, j: (0, j))],
        out_specs=(tile, tile, pl.BlockSpec((ROW_TILE, 1), lambda i, j: (i, 0))),
        out_shape=_residual_out(m, n),
        compiler_params=_params("parallel", "arbitrary"),
        name="out_proj",
    )(ya, yb, w, x, next_gain.reshape(1, n).astype(F32))


def _mlp_up_kernel(a_ref, w_ref, ssq_ref, o_ref):
    acc = jnp.dot(a_ref[...], w_ref[...], preferred_element_type=F32)
    pre = acc * _row_inv_rms(ssq_ref, a_ref.shape[1])
    o_ref[...] = jnp.square(jnp.maximum(pre, 0.0)).astype(o_ref.dtype)


def _mlp_up(a, ssq, w):
    m, k = a.shape
    n = w.shape[1]
    return pl.pallas_call(
        _mlp_up_kernel,
        grid=(m // ROW_TILE, n // COL_TILE),
        in_specs=[pl.BlockSpec((ROW_TILE, k), lambda i, j: (i, 0)),
                  pl.BlockSpec((k, COL_TILE), lambda i, j: (0, j)),
                  pl.BlockSpec((ROW_TILE, 1), lambda i, j: (i, 0))],
        out_specs=pl.BlockSpec((ROW_TILE, COL_TILE), lambda i, j: (i, j)),
        out_shape=jax.ShapeDtypeStruct((m, n), BF16),
        compiler_params=_params("parallel", "parallel"),
        name="mlp_up",
    )(a, w, ssq)


def _mlp_down_kernel(a_ref, w_ref, r_ref, gain_ref, h_ref, hg_ref, ssq_ref, acc_ref):
    kk = pl.program_id(2)
    last = pl.num_programs(2) - 1

    def partial_product():
        return jnp.dot(a_ref[...], w_ref[...], preferred_element_type=F32)

    @pl.when(kk == 0)
    def _():
        acc_ref[...] = partial_product()

    @pl.when(jnp.logical_and(kk > 0, kk < last))
    def _():
        acc_ref[...] += partial_product()

    @pl.when(kk == last)
    def _():
        _store_residual(r_ref[...] + (acc_ref[...] + partial_product()), pl.program_id(1) == 0,
                        gain_ref, h_ref, hg_ref, ssq_ref)


def _mlp_down(a, w, r, next_gain):
    m, k = a.shape
    n = w.shape[1]
    assert k // DOWN_K_TILE >= 2, "first and last contraction steps must be distinct"
    tile = pl.BlockSpec((ROW_TILE, COL_TILE), lambda i, j, kk: (i, j))
    return pl.pallas_call(
        _mlp_down_kernel,
        grid=(m // ROW_TILE, n // COL_TILE, k // DOWN_K_TILE),
        in_specs=[pl.BlockSpec((ROW_TILE, DOWN_K_TILE), lambda i, j, kk: (i, kk)),
                  pl.BlockSpec((DOWN_K_TILE, COL_TILE), lambda i, j, kk: (kk, j)),
                  tile,
                  pl.BlockSpec((1, COL_TILE), lambda i, j, kk: (0, j))],
        out_specs=(tile, tile, pl.BlockSpec((ROW_TILE, 1), lambda i, j, kk: (i, 0))),
        out_shape=_residual_out(m, n),
        scratch_shapes=[pltpu.VMEM((ROW_TILE, COL_TILE), F32)],
        compiler_params=_params("parallel", "arbitrary", "arbitrary"),
        name="mlp_down",
    )(a, w, r, next_gain.reshape(1, n).astype(F32))


def _ple_kernel(a_ref, wg_ref, ssq_ref, p_ref, wp_ref, h_ref, gf_ref, o_ref, ssq_out_ref):
    j = pl.program_id(1)
    logits = jnp.dot(a_ref[...], wg_ref[...], preferred_element_type=F32)
    gate = 0.5 * jnp.tanh(logits * (0.5 * _row_inv_rms(ssq_ref, a_ref.shape[1]))) + 0.5
    e = jnp.dot(p_ref[...].astype(BF16), wp_ref[...], preferred_element_type=F32)
    h = h_ref[...] + gate * e
    bn = h.shape[1]
    for jj in range(o_ref.shape[1] // bn):
        @pl.when(j == jj)
        def _(jj=jj):
            o_ref[:, jj * bn:(jj + 1) * bn] = h

    @pl.when(j == 0)
    def _():
        ssq_out_ref[...] = jnp.zeros_like(ssq_out_ref)

    ssq_out_ref[...] += jnp.sum(h * h, axis=-1, keepdims=True)

    @pl.when(j == pl.num_programs(1) - 1)
    def _():
        inv = _row_inv_rms(ssq_out_ref, o_ref.shape[1])
        for jj in range(o_ref.shape[1] // bn):
            cols = slice(jj * bn, (jj + 1) * bn)
            o_ref[:, cols] = o_ref[:, cols] * inv * gf_ref[:, cols]


def _ple_final(a, ssq, w_gate, p, w_proj, h, final_gain):
    m, k = a.shape
    n = w_gate.shape[1]
    kp = p.shape[1]
    return pl.pallas_call(
        _ple_kernel,
        grid=(m // PLE_ROW_TILE, n // COL_TILE),
        in_specs=[pl.BlockSpec((PLE_ROW_TILE, k), lambda i, j: (i, 0)),
                  pl.BlockSpec((k, COL_TILE), lambda i, j: (0, j)),
                  pl.BlockSpec((PLE_ROW_TILE, 1), lambda i, j: (i, 0)),
                  pl.BlockSpec((PLE_ROW_TILE, kp), lambda i, j: (i, 0)),
                  pl.BlockSpec((kp, COL_TILE), lambda i, j: (0, j)),
                  pl.BlockSpec((PLE_ROW_TILE, COL_TILE), lambda i, j: (i, j)),
                  pl.BlockSpec((1, n), lambda i, j: (0, 0))],
        out_specs=pl.BlockSpec((PLE_ROW_TILE, n), lambda i, j: (i, 0)),
        out_shape=jax.ShapeDtypeStruct((m, n), F32),
        scratch_shapes=[pltpu.VMEM((PLE_ROW_TILE, 1), F32)],
        compiler_params=_params("parallel", "arbitrary"),
        name="ple_gate",
    )(a, w_gate, ssq, p, w_proj, h, final_gain.reshape(1, n).astype(F32))


def kernel(x, p, norm_mix, w_in, w_s, b_s, norm_a_out, norm_b_out, w_out, norm_ffn, w_up,
           w_down, norm_ple, w_ple_gate, w_ple_proj, norm_final):
    batch, seq, d_model = x.shape
    assert w_in.shape[0] == 1, "single layer: the gated-embedding kernel applies the final rmsnorm"
    a_width = norm_a_out.shape[1]
    b_width = norm_b_out.shape[1]
    sgu_heads = a_width // HEAD_DIM
    sb_heads = b_width // HEAD_DIM
    m = batch * seq
    scale = 1.0 / math.sqrt(HEAD_DIM)
    o1, o2 = a_width, 2 * a_width
    o3, o4 = o2 + b_width, o2 + 2 * b_width

    h = x.reshape(m, d_model)
    a = _rmsnorm(h, norm_mix[0], BF16)
    z = _in_proj(a, w_in[0].astype(BF16), o2, o3, scale)
    y_a = _sgu(z, w_s[0], b_s[0], norm_a_out[0], sgu_heads, 0, o1)
    y_b = _attention(z.reshape(batch, seq, -1), norm_b_out[0], sb_heads, o2, o3, o4)
    h, hg, ssq = _out_proj(y_a, y_b.reshape(m, b_width), w_out[0].astype(BF16), h, norm_ffn[0])
    hid = _mlp_up(hg, ssq, w_up[0].astype(BF16))
    h, hg, ssq = _mlp_down(hid, w_down[0].astype(BF16), h, norm_ple[0])
    out = _ple_final(hg, ssq, w_ple_gate[0].astype(BF16), p[0].reshape(m, PLE_DIM),
                     w_ple_proj[0].astype(BF16), h, norm_final)
    return out.reshape(batch, seq, d_model).astype(x.dtype)
```

```python
import functools
import math

import jax
import jax.numpy as jnp
from jax import lax
from jax.experimental import pallas as pl
from jax.experimental.pallas import tpu as pltpu

F32 = jnp.float32
BF16 = jnp.bfloat16

EPS = 1e-6
HEAD_DIM = 128
CHUNK = 128
PLE_DIM = 256

ROW_TILE = 1024
COL_TILE = 1024
OUT_COL_TILE = 512
PLE_ROW_TILE = 512
DOWN_K_TILE = 2048
NORM_ROWS = 256
SGU_ROWS = 1024
ATTN_BLOCK = 256
ATTN_HEADS = 4
ATTN_SKIP_LOG = -110.0
VMEM_LIMIT = 56 * 1024 * 1024


def _params(*sem):
    return pltpu.CompilerParams(dimension_semantics=sem, vmem_limit_bytes=VMEM_LIMIT)


def _rmsnorm_kernel(x_ref, g_ref, o_ref):
    x = x_ref[...].astype(F32)
    inv = lax.rsqrt(jnp.mean(x * x, axis=-1, keepdims=True) + EPS)
    o_ref[...] = (x * inv * g_ref[...]).astype(o_ref.dtype)


def _rmsnorm(x, g, out_dtype):
    m, d = x.shape
    return pl.pallas_call(
        _rmsnorm_kernel,
        grid=(m // NORM_ROWS,),
        in_specs=[pl.BlockSpec((NORM_ROWS, d), lambda i: (i, 0)),
                  pl.BlockSpec((1, d), lambda i: (0, 0))],
        out_specs=pl.BlockSpec((NORM_ROWS, d), lambda i: (i, 0)),
        out_shape=jax.ShapeDtypeStruct((m, d), out_dtype),
        compiler_params=_params("parallel"),
        name="rmsnorm",
    )(x, g.reshape(1, d).astype(F32))


def _in_proj_kernel(a_ref, w_ref, c0_ref, c1_ref, o_ref, c0_out_ref, c1_out_ref, *,
                    q_lo, q_hi, scale):
    j = pl.program_id(1)
    _cast_rider(c0_ref, c0_out_ref)
    _cast_rider(c1_ref, c1_out_ref)
    acc = jnp.dot(a_ref[...], w_ref[...], preferred_element_type=F32)
    s = jnp.where(jnp.logical_and(j >= q_lo, j < q_hi), scale, 1.0).astype(F32)
    o_ref[...] = (acc * s).astype(o_ref.dtype)


def _cast_rider(src_ref, dst_ref):
    dst_ref[...] = src_ref[...].astype(dst_ref.dtype)


def _rider_spec(w, n_i, j_lo, j_hi):
    rows, cols = w.shape[0] // n_i, w.shape[1] // (j_hi - j_lo)
    assert rows * n_i == w.shape[0] and cols * (j_hi - j_lo) == w.shape[1]
    return pl.BlockSpec((rows, cols), lambda i, j: (i, jnp.clip(j - j_lo, 0, j_hi - j_lo - 1)))


def _in_proj(a, w, q_col_lo, q_col_hi, scale, cast0, cast1):
    m, k = a.shape
    n = w.shape[1]
    n_i, n_j = m // ROW_TILE, n // COL_TILE
    split = n_j - max(1, n_j // 5)
    kern = functools.partial(_in_proj_kernel, q_lo=q_col_lo // COL_TILE,
                             q_hi=q_col_hi // COL_TILE, scale=scale)
    riders = [_rider_spec(cast0, n_i, 0, split), _rider_spec(cast1, n_i, split, n_j)]
    return pl.pallas_call(
        kern,
        grid=(n_i, n_j),
        in_specs=[pl.BlockSpec((ROW_TILE, k), lambda i, j: (i, 0)),
                  pl.BlockSpec((k, COL_TILE), lambda i, j: (0, j))] + riders,
        out_specs=[pl.BlockSpec((ROW_TILE, COL_TILE), lambda i, j: (i, j))] + riders,
        out_shape=(jax.ShapeDtypeStruct((m, n), BF16),
                   jax.ShapeDtypeStruct(cast0.shape, BF16),
                   jax.ShapeDtypeStruct(cast1.shape, BF16)),
        compiler_params=_params("parallel", "arbitrary"),
        name="in_proj",
    )(a, w, cast0, cast1)


def _sgu_kernel(u_ref, v_ref, w_ref, b_ref, g_ref, o_ref):
    u = jax.nn.gelu(u_ref[...].astype(F32))
    v = jax.nn.gelu(v_ref[...].astype(F32))
    mu = jnp.mean(v, axis=-1, keepdims=True)
    vc = v - mu
    var = jnp.mean(vc * vc, axis=-1, keepdims=True)
    vn = (vc * lax.rsqrt(var + EPS)).astype(BF16)
    row = lax.broadcasted_iota(jnp.int32, (CHUNK, CHUNK), 0)
    col = lax.broadcasted_iota(jnp.int32, (CHUNK, CHUNK), 1)
    w = jnp.where(row >= col, w_ref[0], 0.0).astype(BF16)
    bias = b_ref[0]
    g = g_ref[0]
    for c in range(u.shape[0] // CHUNK):
        sl = slice(c * CHUNK, (c + 1) * CHUNK)
        mixed = jnp.dot(w, vn[sl], preferred_element_type=F32) + bias
        y = u[sl] * mixed
        inv = lax.rsqrt(jnp.mean(y * y, axis=-1, keepdims=True) + EPS)
        o_ref[sl, :] = (y * inv * g).astype(o_ref.dtype)


def _sgu(z, w_s, b_s, gain, n_heads, u_col0, v_col0):
    m = z.shape[0]
    ub, vb = u_col0 // HEAD_DIM, v_col0 // HEAD_DIM
    bias = jnp.broadcast_to(b_s.astype(F32)[:, :, None], (n_heads, CHUNK, HEAD_DIM))
    return pl.pallas_call(
        _sgu_kernel,
        grid=(m // SGU_ROWS, n_heads),
        in_specs=[pl.BlockSpec((SGU_ROWS, HEAD_DIM), lambda i, h: (i, ub + h)),
                  pl.BlockSpec((SGU_ROWS, HEAD_DIM), lambda i, h: (i, vb + h)),
                  pl.BlockSpec((1, CHUNK, CHUNK), lambda i, h: (h, 0, 0)),
                  pl.BlockSpec((1, CHUNK, HEAD_DIM), lambda i, h: (h, 0, 0)),
                  pl.BlockSpec((1, 1, HEAD_DIM), lambda i, h: (h, 0, 0))],
        out_specs=pl.BlockSpec((SGU_ROWS, HEAD_DIM), lambda i, h: (i, h)),
        out_shape=jax.ShapeDtypeStruct((m, n_heads * HEAD_DIM), BF16),
        compiler_params=_params("parallel", "parallel"),
        name="sgu",
    )(z, z, w_s.astype(F32), bias, gain.reshape(n_heads, 1, HEAD_DIM).astype(F32))


def _attn_kernel(q_ref, k_ref, v_ref, g_ref, o_ref, acc_ref, carry_ref):
    blk = ATTN_BLOCK
    heads = range(q_ref.shape[2] // HEAD_DIM)
    qi = pl.program_id(2)
    row = lax.broadcasted_iota(jnp.int32, (blk, blk), 0)
    col = lax.broadcasted_iota(jnp.int32, (blk, blk), 1)
    after = jnp.where(row > col, 1.0, 0.0).astype(BF16)
    after2 = jnp.concatenate([after, after], axis=0)
    causal = col < row

    def lanes(h):
        return slice(h * HEAD_DIM, (h + 1) * HEAD_DIM)

    def scores(h, kb):
        k = k_ref[0, pl.ds(kb * blk, blk), lanes(h)]
        return lax.dot_general(q_ref[0, :, lanes(h)], k, (((1,), (1,)), ((), ())),
                               preferred_element_type=F32)

    def log_terms(z, masked):
        t = jnp.log(1.0 + jnp.exp(-jnp.abs(z)))
        log_beta = jnp.minimum(z, 0.0) - t
        log_keep = log_beta - z
        if masked:
            log_keep = jnp.where(causal, log_keep, 0.0)
        hi = log_keep.astype(BF16)
        lo = (log_keep - hi.astype(F32)).astype(BF16)
        return log_beta, log_keep, jnp.concatenate([hi, lo], axis=1)

    def attend(tiles, first):
        z = [[scores(h, kb) for kb, _ in tiles] for h in heads]
        logs = [[log_terms(z[h][t], tiles[t][1]) for t in range(len(tiles))] for h in heads]
        sums = [[jnp.dot(lg[2], after2, preferred_element_type=F32) for lg in logs[h]]
                for h in heads]
        weights = []
        for h in heads:
            carry = None if first else carry_ref[h]
            row_w = []
            for t, (_, masked) in enumerate(tiles):
                log_beta, log_keep, _ = logs[h][t]
                log_after = sums[h][t] if carry is None else sums[h][t] + carry
                a = jnp.exp(log_beta + log_after)
                if masked:
                    a = jnp.where(causal, a, 0.0)
                row_w.append(a.astype(BF16))
                block_sum = jnp.sum(log_keep, axis=-1, keepdims=True)
                carry = block_sum if carry is None else carry + block_sum
            carry_ref[h] = carry
            weights.append(row_w)
        for h in heads:
            out = None
            for t, (kb, _) in enumerate(tiles):
                v = v_ref[0, pl.ds(kb * blk, blk), lanes(h)]
                part = jnp.dot(weights[h][t], v, preferred_element_type=F32)
                out = part if out is None else out + part
            acc_ref[h] = out if first else acc_ref[h] + out

    def live():
        top = carry_ref[0]
        for h in heads[1:]:
            top = jnp.maximum(top, carry_ref[h])
        return (jnp.max(top) > ATTN_SKIP_LOG).astype(jnp.int32)

    @pl.when(qi == 0)
    def _():
        attend([(qi, True)], True)

    @pl.when(qi > 0)
    def _():
        attend([(qi, True), (qi - 1, False)], True)

        def cond(state):
            return jnp.logical_and(state[0] >= 0, state[1] > 0)

        def body(state):
            attend([(state[0], False)], False)
            return state[0] - 1, live()

        lax.while_loop(cond, body, (qi - 2, live()))

    for h in heads:
        acc = acc_ref[h]
        inv = lax.rsqrt(jnp.mean(acc * acc, axis=-1, keepdims=True) + EPS)
        o_ref[0, :, lanes(h)] = (acc * inv * g_ref[0, :, lanes(h)]).astype(o_ref.dtype)


def _attention(z, gain, n_heads, q_col0, k_col0, v_col0):
    b, s, _ = z.shape
    width = ATTN_HEADS * HEAD_DIM
    qb, kb, vb = q_col0 // width, k_col0 // width, v_col0 // width
    return pl.pallas_call(
        _attn_kernel,
        grid=(b, n_heads // ATTN_HEADS, s // ATTN_BLOCK),
        in_specs=[pl.BlockSpec((1, ATTN_BLOCK, width), lambda bi, h, qi: (bi, qi, qb + h)),
                  pl.BlockSpec((1, s, width), lambda bi, h, qi: (bi, 0, kb + h)),
                  pl.BlockSpec((1, s, width), lambda bi, h, qi: (bi, 0, vb + h)),
                  pl.BlockSpec((1, 1, width), lambda bi, h, qi: (0, 0, h))],
        out_specs=pl.BlockSpec((1, ATTN_BLOCK, width), lambda bi, h, qi: (bi, qi, h)),
        out_shape=jax.ShapeDtypeStruct((b, s, n_heads * HEAD_DIM), BF16),
        scratch_shapes=[pltpu.VMEM((ATTN_HEADS, ATTN_BLOCK, HEAD_DIM), F32),
                        pltpu.VMEM((ATTN_HEADS, ATTN_BLOCK, 1), F32)],
        compiler_params=_params("parallel", "parallel", "arbitrary"),
        name="stick_breaking",
    )(z, z, z, gain.reshape(1, 1, n_heads * HEAD_DIM).astype(F32))


def _store_residual(h, first_col_tile, gain_ref, h_ref, hg_ref, ssq_ref):
    h_ref[...] = h
    hg_ref[...] = (h * gain_ref[...]).astype(hg_ref.dtype)

    @pl.when(first_col_tile)
    def _():
        ssq_ref[...] = jnp.zeros_like(ssq_ref)

    ssq_ref[...] += jnp.sum(h * h, axis=-1, keepdims=True)


def _row_inv_rms(ssq_ref, width):
    return lax.rsqrt(ssq_ref[...] * (1.0 / width) + EPS)


def _residual_out(m, n):
    return (jax.ShapeDtypeStruct((m, n), F32), jax.ShapeDtypeStruct((m, n), BF16),
            jax.ShapeDtypeStruct((m, 1), F32))


def _out_proj_kernel(ya_ref, yb_ref, w_ref, x_ref, gain_ref, h_ref, hg_ref, ssq_ref):
    ka = ya_ref.shape[1]
    acc = jnp.dot(ya_ref[...], w_ref[:ka, :], preferred_element_type=F32)
    acc = acc + jnp.dot(yb_ref[...], w_ref[ka:, :], preferred_element_type=F32)
    _store_residual(x_ref[...] + acc, pl.program_id(1) == 0, gain_ref, h_ref, hg_ref, ssq_ref)


def _out_proj(ya, yb, w, x, next_gain):
    m, ka = ya.shape
    kb = yb.shape[1]
    n = w.shape[1]
    tile = pl.BlockSpec((ROW_TILE, OUT_COL_TILE), lambda i, j: (i, j))
    return pl.pallas_call(
        _out_proj_kernel,
        grid=(m // ROW_TILE, n // OUT_COL_TILE),
        in_specs=[pl.BlockSpec((ROW_TILE, ka), lambda i, j: (i, 0)),
                  pl.BlockSpec((ROW_TILE, kb), lambda i, j: (i, 0)),
                  pl.BlockSpec((ka + kb, OUT_COL_TILE), lambda i, j: (0, j)),
                  tile,
                  pl.BlockSpec((1, OUT_COL_TILE), lambda i, j: (0, j))],
        out_specs=(tile, tile, pl.BlockSpec((ROW_TILE, 1), lambda i, j: (i, 0))),
        out_shape=_residual_out(m, n),
        compiler_params=_params("parallel", "arbitrary"),
        name="out_proj",
    )(ya, yb, w, x, next_gain.reshape(1, n).astype(F32))


def _mlp_up_kernel(a_ref, w_ref, ssq_ref, c0_ref, c1_ref, o_ref, c0_out_ref, c1_out_ref):
    _cast_rider(c0_ref, c0_out_ref)
    _cast_rider(c1_ref, c1_out_ref)
    acc = jnp.dot(a_ref[...], w_ref[...], preferred_element_type=F32)
    pre = acc * _row_inv_rms(ssq_ref, a_ref.shape[1])
    o_ref[...] = jnp.square(jnp.maximum(pre, 0.0)).astype(o_ref.dtype)


def _mlp_up(a, ssq, w, cast0, cast1):
    m, k = a.shape
    n = w.shape[1]
    n_i, n_j = m // ROW_TILE, n // COL_TILE
    split = n_j // 2
    riders = [_rider_spec(cast0, n_i, 0, split), _rider_spec(cast1, n_i, split, n_j)]
    return pl.pallas_call(
        _mlp_up_kernel,
        grid=(n_i, n_j),
        in_specs=[pl.BlockSpec((ROW_TILE, k), lambda i, j: (i, 0)),
                  pl.BlockSpec((k, COL_TILE), lambda i, j: (0, j)),
                  pl.BlockSpec((ROW_TILE, 1), lambda i, j: (i, 0))] + riders,
        out_specs=[pl.BlockSpec((ROW_TILE, COL_TILE), lambda i, j: (i, j))] + riders,
        out_shape=(jax.ShapeDtypeStruct((m, n), BF16),
                   jax.ShapeDtypeStruct(cast0.shape, BF16),
                   jax.ShapeDtypeStruct(cast1.shape, BF16)),
        compiler_params=_params("parallel", "arbitrary"),
        name="mlp_up",
    )(a, w, ssq, cast0, cast1)


def _mlp_down_kernel(a_ref, w_ref, r_ref, gain_ref, h_ref, hg_ref, ssq_ref, acc_ref):
    kk = pl.program_id(2)
    last = pl.num_programs(2) - 1

    def partial_product():
        return jnp.dot(a_ref[...], w_ref[...], preferred_element_type=F32)

    @pl.when(kk == 0)
    def _():
        acc_ref[...] = partial_product()

    @pl.when(jnp.logical_and(kk > 0, kk < last))
    def _():
        acc_ref[...] += partial_product()

    @pl.when(kk == last)
    def _():
        _store_residual(r_ref[...] + (acc_ref[...] + partial_product()), pl.program_id(1) == 0,
                        gain_ref, h_ref, hg_ref, ssq_ref)


def _mlp_down(a, w, r, next_gain):
    m, k = a.shape
    n = w.shape[1]
    assert k // DOWN_K_TILE >= 2, "first and last contraction steps must be distinct"
    tile = pl.BlockSpec((ROW_TILE, COL_TILE), lambda i, j, kk: (i, j))
    return pl.pallas_call(
        _mlp_down_kernel,
        grid=(m // ROW_TILE, n // COL_TILE, k // DOWN_K_TILE),
        in_specs=[pl.BlockSpec((ROW_TILE, DOWN_K_TILE), lambda i, j, kk: (i, kk)),
                  pl.BlockSpec((DOWN_K_TILE, COL_TILE), lambda i, j, kk: (kk, j)),
                  tile,
                  pl.BlockSpec((1, COL_TILE), lambda i, j, kk: (0, j))],
        out_specs=(tile, tile, pl.BlockSpec((ROW_TILE, 1), lambda i, j, kk: (i, 0))),
        out_shape=_residual_out(m, n),
        scratch_shapes=[pltpu.VMEM((ROW_TILE, COL_TILE), F32)],
        compiler_params=_params("parallel", "arbitrary", "arbitrary"),
        name="mlp_down",
    )(a, w, r, next_gain.reshape(1, n).astype(F32))


def _ple_kernel(a_ref, wg_ref, ssq_ref, p_ref, wp_ref, h_ref, gf_ref, o_ref, ssq_out_ref):
    j = pl.program_id(1)
    logits = jnp.dot(a_ref[...], wg_ref[...], preferred_element_type=F32)
    gate = 0.5 * jnp.tanh(logits * (0.5 * _row_inv_rms(ssq_ref, a_ref.shape[1]))) + 0.5
    e = jnp.dot(p_ref[...].astype(BF16), wp_ref[...], preferred_element_type=F32)
    h = h_ref[...] + gate * e
    bn = h.shape[1]
    o_ref[:, pl.ds(pl.multiple_of(j * bn, bn), bn)] = h

    @pl.when(j == 0)
    def _():
        ssq_out_ref[...] = jnp.zeros_like(ssq_out_ref)

    ssq_out_ref[...] += jnp.sum(h * h, axis=-1, keepdims=True)

    @pl.when(j == pl.num_programs(1) - 1)
    def _():
        inv = _row_inv_rms(ssq_out_ref, o_ref.shape[1])
        for jj in range(o_ref.shape[1] // bn):
            cols = slice(jj * bn, (jj + 1) * bn)
            o_ref[:, cols] = o_ref[:, cols] * inv * gf_ref[:, cols]


def _ple_final(a, ssq, w_gate, p, w_proj, h, final_gain):
    m, k = a.shape
    n = w_gate.shape[1]
    kp = p.shape[1]
    return pl.pallas_call(
        _ple_kernel,
        grid=(m // PLE_ROW_TILE, n // COL_TILE),
        in_specs=[pl.BlockSpec((PLE_ROW_TILE, k), lambda i, j: (i, 0)),
                  pl.BlockSpec((k, COL_TILE), lambda i, j: (0, j)),
                  pl.BlockSpec((PLE_ROW_TILE, 1), lambda i, j: (i, 0)),
                  pl.BlockSpec((PLE_ROW_TILE, kp), lambda i, j: (i, 0)),
                  pl.BlockSpec((kp, COL_TILE), lambda i, j: (0, j)),
                  pl.BlockSpec((PLE_ROW_TILE, COL_TILE), lambda i, j: (i, j)),
                  pl.BlockSpec((1, n), lambda i, j: (0, 0))],
        out_specs=pl.BlockSpec((PLE_ROW_TILE, n), lambda i, j: (i, 0)),
        out_shape=jax.ShapeDtypeStruct((m, n), F32),
        scratch_shapes=[pltpu.VMEM((PLE_ROW_TILE, 1), F32)],
        compiler_params=_params("parallel", "arbitrary"),
        name="ple_gate",
    )(a, w_gate, ssq, p, w_proj, h, final_gain.reshape(1, n).astype(F32))


def kernel(x, p, norm_mix, w_in, w_s, b_s, norm_a_out, norm_b_out, w_out, norm_ffn, w_up,
           w_down, norm_ple, w_ple_gate, w_ple_proj, norm_final):
    batch, seq, d_model = x.shape
    assert w_in.shape[0] == 1, "single layer: the gated-embedding kernel applies the final rmsnorm"
    a_width = norm_a_out.shape[1]
    b_width = norm_b_out.shape[1]
    sgu_heads = a_width // HEAD_DIM
    sb_heads = b_width // HEAD_DIM
    m = batch * seq
    scale = 1.0 / math.sqrt(HEAD_DIM)
    o1, o2 = a_width, 2 * a_width
    o3, o4 = o2 + b_width, o2 + 2 * b_width

    h = x.reshape(m, d_model)
    a = _rmsnorm(h, norm_mix[0], BF16)
    z, w_up_bf, w_out_bf = _in_proj(a, w_in[0].astype(BF16), o2, o3, scale, w_up[0], w_out[0])
    y_a = _sgu(z, w_s[0], b_s[0], norm_a_out[0], sgu_heads, 0, o1)
    y_b = _attention(z.reshape(batch, seq, -1), norm_b_out[0], sb_heads, o2, o3, o4)
    h, hg, ssq = _out_proj(y_a, y_b.reshape(m, b_width), w_out_bf, h, norm_ffn[0])
    hid, w_down_bf, w_gate_bf = _mlp_up(hg, ssq, w_up_bf, w_down[0], w_ple_gate[0])
    h, hg, ssq = _mlp_down(hid, w_down_bf, h, norm_ple[0])
    out = _ple_final(hg, ssq, w_gate_bf, p[0].reshape(m, PLE_DIM),
                     w_ple_proj[0].astype(BF16), h, norm_final)
    return out.reshape(batch, seq, d_model).astype(x.dtype)
```

```python
import functools
import math

import jax
import jax.numpy as jnp
from jax import lax
from jax.experimental import pallas as pl
from jax.experimental.pallas import tpu as pltpu

F32 = jnp.float32
BF16 = jnp.bfloat16

EPS = 1e-6
HEAD_DIM = 128
CHUNK = 128
PLE_DIM = 256

ROW_TILE = 1024
COL_TILE = 1024
OUT_COL_TILE = 512
PLE_ROW_TILE = 512
DOWN_K_TILE = 2048
NORM_ROWS = 256
SGU_ROWS = 2048
ATTN_BLOCK = 256
ATTN_HEADS = 4
ATTN_SKIP_LOG2 = -160.0
MASKED_LOGIT = -1e30
LOG2E = 1.4426950408889634
VMEM_LIMIT = 56 * 1024 * 1024


def _params(*sem):
    return pltpu.CompilerParams(dimension_semantics=sem, vmem_limit_bytes=VMEM_LIMIT)


def _rmsnorm_kernel(x_ref, g_ref, o_ref):
    x = x_ref[...].astype(F32)
    inv = lax.rsqrt(jnp.mean(x * x, axis=-1, keepdims=True) + EPS)
    o_ref[...] = (x * inv * g_ref[...]).astype(o_ref.dtype)


def _rmsnorm(x, g, out_dtype):
    m, d = x.shape
    return pl.pallas_call(
        _rmsnorm_kernel,
        grid=(m // NORM_ROWS,),
        in_specs=[pl.BlockSpec((NORM_ROWS, d), lambda i: (i, 0)),
                  pl.BlockSpec((1, d), lambda i: (0, 0))],
        out_specs=pl.BlockSpec((NORM_ROWS, d), lambda i: (i, 0)),
        out_shape=jax.ShapeDtypeStruct((m, d), out_dtype),
        compiler_params=_params("parallel"),
        name="rmsnorm",
    )(x, g.reshape(1, d).astype(F32))


def _in_proj_kernel(a_ref, w_ref, c0_ref, c1_ref, o_ref, c0_out_ref, c1_out_ref, *,
                    q_lo, q_hi, scale):
    j = pl.program_id(1)
    _cast_rider(c0_ref, c0_out_ref)
    _cast_rider(c1_ref, c1_out_ref)
    acc = jnp.dot(a_ref[...], w_ref[...], preferred_element_type=F32)
    s = jnp.where(jnp.logical_and(j >= q_lo, j < q_hi), scale, 1.0).astype(F32)
    o_ref[...] = (acc * s).astype(o_ref.dtype)


def _cast_rider(src_ref, dst_ref):
    dst_ref[...] = src_ref[...].astype(dst_ref.dtype)


def _rider_spec(w, n_i, j_lo, j_hi):
    rows, cols = w.shape[0] // n_i, w.shape[1] // (j_hi - j_lo)
    assert rows * n_i == w.shape[0] and cols * (j_hi - j_lo) == w.shape[1]
    return pl.BlockSpec((rows, cols), lambda i, j: (i, jnp.clip(j - j_lo, 0, j_hi - j_lo - 1)))


def _in_proj(a, w, q_col_lo, q_col_hi, scale, cast0, cast1):
    m, k = a.shape
    n = w.shape[1]
    n_i, n_j = m // ROW_TILE, n // COL_TILE
    split = n_j - max(1, n_j // 5)
    kern = functools.partial(_in_proj_kernel, q_lo=q_col_lo // COL_TILE,
                             q_hi=q_col_hi // COL_TILE, scale=scale)
    riders = [_rider_spec(cast0, n_i, 0, split), _rider_spec(cast1, n_i, split, n_j)]
    return pl.pallas_call(
        kern,
        grid=(n_i, n_j),
        in_specs=[pl.BlockSpec((ROW_TILE, k), lambda i, j: (i, 0)),
                  pl.BlockSpec((k, COL_TILE), lambda i, j: (0, j))] + riders,
        out_specs=[pl.BlockSpec((ROW_TILE, COL_TILE), lambda i, j: (i, j))] + riders,
        out_shape=(jax.ShapeDtypeStruct((m, n), BF16),
                   jax.ShapeDtypeStruct(cast0.shape, BF16),
                   jax.ShapeDtypeStruct(cast1.shape, BF16)),
        compiler_params=_params("parallel", "arbitrary"),
        name="in_proj",
    )(a, w, cast0, cast1)


def _sgu_kernel(u_ref, v_ref, w_ref, b_ref, g_ref, o_ref):
    u = jax.nn.gelu(u_ref[...].astype(F32))
    v = jax.nn.gelu(v_ref[...].astype(F32))
    mu = jnp.mean(v, axis=-1, keepdims=True)
    vc = v - mu
    var = jnp.mean(vc * vc, axis=-1, keepdims=True)
    vn = (vc * lax.rsqrt(var + EPS)).astype(BF16)
    row = lax.broadcasted_iota(jnp.int32, (CHUNK, CHUNK), 0)
    col = lax.broadcasted_iota(jnp.int32, (CHUNK, CHUNK), 1)
    w = jnp.where(row >= col, w_ref[0], 0.0).astype(BF16)
    bias = b_ref[0]
    g = g_ref[0]
    for c in range(u.shape[0] // CHUNK):
        sl = slice(c * CHUNK, (c + 1) * CHUNK)
        mixed = jnp.dot(w, vn[sl], preferred_element_type=F32) + bias
        y = u[sl] * mixed
        inv = lax.rsqrt(jnp.mean(y * y, axis=-1, keepdims=True) + EPS)
        o_ref[sl, :] = (y * inv * g).astype(o_ref.dtype)


def _sgu(z, w_s, b_s, gain, n_heads, u_col0, v_col0):
    m = z.shape[0]
    ub, vb = u_col0 // HEAD_DIM, v_col0 // HEAD_DIM
    bias = jnp.broadcast_to(b_s.astype(F32)[:, :, None], (n_heads, CHUNK, HEAD_DIM))
    return pl.pallas_call(
        _sgu_kernel,
        grid=(m // SGU_ROWS, n_heads),
        in_specs=[pl.BlockSpec((SGU_ROWS, HEAD_DIM), lambda i, h: (i, ub + h)),
                  pl.BlockSpec((SGU_ROWS, HEAD_DIM), lambda i, h: (i, vb + h)),
                  pl.BlockSpec((1, CHUNK, CHUNK), lambda i, h: (h, 0, 0)),
                  pl.BlockSpec((1, CHUNK, HEAD_DIM), lambda i, h: (h, 0, 0)),
                  pl.BlockSpec((1, 1, HEAD_DIM), lambda i, h: (h, 0, 0))],
        out_specs=pl.BlockSpec((SGU_ROWS, HEAD_DIM), lambda i, h: (i, h)),
        out_shape=jax.ShapeDtypeStruct((m, n_heads * HEAD_DIM), BF16),
        compiler_params=_params("parallel", "parallel"),
        name="sgu",
    )(z, z, w_s.astype(F32), bias, gain.reshape(n_heads, 1, HEAD_DIM).astype(F32))


def _attn_kernel(q_ref, k_ref, v_ref, g_ref, o_ref, acc_ref, carry_ref):
    blk = ATTN_BLOCK
    heads = range(q_ref.shape[2] // HEAD_DIM)
    qi = pl.program_id(2)
    row = lax.broadcasted_iota(jnp.int32, (blk, blk), 0)
    col = lax.broadcasted_iota(jnp.int32, (blk, blk), 1)
    after = jnp.where(row > col, 1.0, 0.0).astype(BF16)
    causal = col < row

    def lanes(h):
        return slice(h * HEAD_DIM, (h + 1) * HEAD_DIM)

    def scores(h, kb):
        k = k_ref[0, pl.ds(kb * blk, blk), lanes(h)]
        return lax.dot_general(q_ref[0, :, lanes(h)], k, (((1,), (1,)), ((), ())),
                               preferred_element_type=F32)

    def log_terms(z, masked):
        if masked:
            z = jnp.where(causal, z, MASKED_LOGIT)
        t = jnp.log(1.0 + jnp.exp2(-jnp.abs(z))) * LOG2E
        log_beta = jnp.minimum(z, 0.0) - t
        return log_beta, log_beta - z

    def attend(tiles, first):
        z = [[scores(h, kb) for kb, _ in tiles] for h in heads]
        logs = [[log_terms(z[h][t], tiles[t][1]) for t in range(len(tiles))] for h in heads]
        sums = [[jnp.dot(lg[1].astype(BF16), after, preferred_element_type=F32) for lg in logs[h]]
                for h in heads]
        weights = []
        for h in heads:
            carry = None if first else carry_ref[h]
            row_w = []
            for t in range(len(tiles)):
                log_beta, log_keep = logs[h][t]
                log_after = sums[h][t] if carry is None else sums[h][t] + carry
                row_w.append(jnp.exp2(log_beta + log_after).astype(BF16))
                block_sum = jnp.sum(log_keep, axis=-1, keepdims=True)
                carry = block_sum if carry is None else carry + block_sum
            carry_ref[h] = carry
            weights.append(row_w)
        for h in heads:
            out = None
            for t, (kb, _) in enumerate(tiles):
                v = v_ref[0, pl.ds(kb * blk, blk), lanes(h)]
                part = jnp.dot(weights[h][t], v, preferred_element_type=F32)
                out = part if out is None else out + part
            acc_ref[h] = out if first else acc_ref[h] + out

    def live():
        top = carry_ref[0]
        for h in heads[1:]:
            top = jnp.maximum(top, carry_ref[h])
        return (jnp.max(top) > ATTN_SKIP_LOG2).astype(jnp.int32)

    @pl.when(qi == 0)
    def _():
        attend([(qi, True)], True)

    @pl.when(qi > 0)
    def _():
        attend([(qi, True), (qi - 1, False)], True)

        def cond(state):
            return jnp.logical_and(state[0] >= 0, state[1] > 0)

        def body(state):
            attend([(state[0], False)], False)
            return state[0] - 1, live()

        lax.while_loop(cond, body, (qi - 2, live()))

    for h in heads:
        acc = acc_ref[h]
        inv = lax.rsqrt(jnp.mean(acc * acc, axis=-1, keepdims=True) + EPS)
        o_ref[0, :, lanes(h)] = (acc * inv * g_ref[0, :, lanes(h)]).astype(o_ref.dtype)


def _attention(z, gain, n_heads, q_col0, k_col0, v_col0):
    b, s, _ = z.shape
    width = ATTN_HEADS * HEAD_DIM
    qb, kb, vb = q_col0 // width, k_col0 // width, v_col0 // width
    return pl.pallas_call(
        _attn_kernel,
        grid=(b, n_heads // ATTN_HEADS, s // ATTN_BLOCK),
        in_specs=[pl.BlockSpec((1, ATTN_BLOCK, width), lambda bi, h, qi: (bi, qi, qb + h)),
                  pl.BlockSpec((1, s, width), lambda bi, h, qi: (bi, 0, kb + h)),
                  pl.BlockSpec((1, s, width), lambda bi, h, qi: (bi, 0, vb + h)),
                  pl.BlockSpec((1, 1, width), lambda bi, h, qi: (0, 0, h))],
        out_specs=pl.BlockSpec((1, ATTN_BLOCK, width), lambda bi, h, qi: (bi, qi, h)),
        out_shape=jax.ShapeDtypeStruct((b, s, n_heads * HEAD_DIM), BF16),
        scratch_shapes=[pltpu.VMEM((ATTN_HEADS, ATTN_BLOCK, HEAD_DIM), F32),
                        pltpu.VMEM((ATTN_HEADS, ATTN_BLOCK, 1), F32)],
        compiler_params=_params("parallel", "parallel", "arbitrary"),
        name="stick_breaking",
    )(z, z, z, gain.reshape(1, 1, n_heads * HEAD_DIM).astype(F32))


def _store_residual(h, first_col_tile, gain_ref, h_ref, hg_ref, ssq_ref):
    h_ref[...] = h
    hg_ref[...] = (h * gain_ref[...]).astype(hg_ref.dtype)

    @pl.when(first_col_tile)
    def _():
        ssq_ref[...] = jnp.zeros_like(ssq_ref)

    ssq_ref[...] += jnp.sum(h * h, axis=-1, keepdims=True)


def _row_inv_rms(ssq_ref, width):
    return lax.rsqrt(ssq_ref[...] * (1.0 / width) + EPS)


def _residual_out(m, n):
    return (jax.ShapeDtypeStruct((m, n), F32), jax.ShapeDtypeStruct((m, n), BF16),
            jax.ShapeDtypeStruct((m, 1), F32))


def _out_proj_kernel(ya_ref, yb_ref, w_ref, x_ref, gain_ref, h_ref, hg_ref, ssq_ref):
    ka = ya_ref.shape[1]
    acc = jnp.dot(ya_ref[...], w_ref[:ka, :], preferred_element_type=F32)
    acc = acc + jnp.dot(yb_ref[...], w_ref[ka:, :], preferred_element_type=F32)
    _store_residual(x_ref[...] + acc, pl.program_id(1) == 0, gain_ref, h_ref, hg_ref, ssq_ref)


def _out_proj(ya, yb, w, x, next_gain):
    m, ka = ya.shape
    kb = yb.shape[1]
    n = w.shape[1]
    tile = pl.BlockSpec((ROW_TILE, OUT_COL_TILE), lambda i, j: (i, j))
    return pl.pallas_call(
        _out_proj_kernel,
        grid=(m // ROW_TILE, n // OUT_COL_TILE),
        in_specs=[pl.BlockSpec((ROW_TILE, ka), lambda i, j: (i, 0)),
                  pl.BlockSpec((ROW_TILE, kb), lambda i, j: (i, 0)),
                  pl.BlockSpec((ka + kb, OUT_COL_TILE), lambda i, j: (0, j)),
                  tile,
                  pl.BlockSpec((1, OUT_COL_TILE), lambda i, j: (0, j))],
        out_specs=(tile, tile, pl.BlockSpec((ROW_TILE, 1), lambda i, j: (i, 0))),
        out_shape=_residual_out(m, n),
        compiler_params=_params("parallel", "arbitrary"),
        name="out_proj",
    )(ya, yb, w, x, next_gain.reshape(1, n).astype(F32))


def _mlp_up_kernel(a_ref, w_ref, ssq_ref, c0_ref, c1_ref, o_ref, c0_out_ref, c1_out_ref):
    _cast_rider(c0_ref, c0_out_ref)
    _cast_rider(c1_ref, c1_out_ref)
    acc = jnp.dot(a_ref[...], w_ref[...], preferred_element_type=F32)
    pre = acc * _row_inv_rms(ssq_ref, a_ref.shape[1])
    o_ref[...] = jnp.square(jnp.maximum(pre, 0.0)).astype(o_ref.dtype)


def _mlp_up(a, ssq, w, cast0, cast1):
    m, k = a.shape
    n = w.shape[1]
    n_i, n_j = m // ROW_TILE, n // COL_TILE
    split = n_j // 2
    riders = [_rider_spec(cast0, n_i, 0, split), _rider_spec(cast1, n_i, split, n_j)]
    return pl.pallas_call(
        _mlp_up_kernel,
        grid=(n_i, n_j),
        in_specs=[pl.BlockSpec((ROW_TILE, k), lambda i, j: (i, 0)),
                  pl.BlockSpec((k, COL_TILE), lambda i, j: (0, j)),
                  pl.BlockSpec((ROW_TILE, 1), lambda i, j: (i, 0))] + riders,
        out_specs=[pl.BlockSpec((ROW_TILE, COL_TILE), lambda i, j: (i, j))] + riders,
        out_shape=(jax.ShapeDtypeStruct((m, n), BF16),
                   jax.ShapeDtypeStruct(cast0.shape, BF16),
                   jax.ShapeDtypeStruct(cast1.shape, BF16)),
        compiler_params=_params("parallel", "arbitrary"),
        name="mlp_up",
    )(a, w, ssq, cast0, cast1)


def _mlp_down_kernel(a_ref, w_ref, r_ref, gain_ref, h_ref, hg_ref, ssq_ref, acc_ref):
    kk = pl.program_id(2)
    last = pl.num_programs(2) - 1

    def partial_product():
        return jnp.dot(a_ref[...], w_ref[...], preferred_element_type=F32)

    @pl.when(kk == 0)
    def _():
        acc_ref[...] = partial_product()

    @pl.when(jnp.logical_and(kk > 0, kk < last))
    def _():
        acc_ref[...] += partial_product()

    @pl.when(kk == last)
    def _():
        _store_residual(r_ref[...] + (acc_ref[...] + partial_product()), pl.program_id(1) == 0,
                        gain_ref, h_ref, hg_ref, ssq_ref)


def _mlp_down(a, w, r, next_gain):
    m, k = a.shape
    n = w.shape[1]
    assert k // DOWN_K_TILE >= 2, "first and last contraction steps must be distinct"
    tile = pl.BlockSpec((ROW_TILE, COL_TILE), lambda i, j, kk: (i, j))
    return pl.pallas_call(
        _mlp_down_kernel,
        grid=(m // ROW_TILE, n // COL_TILE, k // DOWN_K_TILE),
        in_specs=[pl.BlockSpec((ROW_TILE, DOWN_K_TILE), lambda i, j, kk: (i, kk)),
                  pl.BlockSpec((DOWN_K_TILE, COL_TILE), lambda i, j, kk: (kk, j)),
                  tile,
                  pl.BlockSpec((1, COL_TILE), lambda i, j, kk: (0, j))],
        out_specs=(tile, tile, pl.BlockSpec((ROW_TILE, 1), lambda i, j, kk: (i, 0))),
        out_shape=_residual_out(m, n),
        scratch_shapes=[pltpu.VMEM((ROW_TILE, COL_TILE), F32)],
        compiler_params=_params("parallel", "arbitrary", "arbitrary"),
        name="mlp_down",
    )(a, w, r, next_gain.reshape(1, n).astype(F32))


def _ple_kernel(a_ref, wg_ref, ssq_ref, p_ref, wp_ref, h_ref, gf_ref, o_ref, ssq_out_ref):
    j = pl.program_id(1)
    logits = jnp.dot(a_ref[...], wg_ref[...], preferred_element_type=F32)
    gate = 0.5 * jnp.tanh(logits * (0.5 * _row_inv_rms(ssq_ref, a_ref.shape[1]))) + 0.5
    e = jnp.dot(p_ref[...].astype(BF16), wp_ref[...], preferred_element_type=F32)
    h = h_ref[...] + gate * e
    bn = h.shape[1]
    o_ref[:, pl.ds(pl.multiple_of(j * bn, bn), bn)] = h

    @pl.when(j == 0)
    def _():
        ssq_out_ref[...] = jnp.zeros_like(ssq_out_ref)

    ssq_out_ref[...] += jnp.sum(h * h, axis=-1, keepdims=True)

    @pl.when(j == pl.num_programs(1) - 1)
    def _():
        inv = _row_inv_rms(ssq_out_ref, o_ref.shape[1])
        for jj in range(o_ref.shape[1] // bn):
            cols = slice(jj * bn, (jj + 1) * bn)
            o_ref[:, cols] = o_ref[:, cols] * inv * gf_ref[:, cols]


def _ple_final(a, ssq, w_gate, p, w_proj, h, final_gain):
    m, k = a.shape
    n = w_gate.shape[1]
    kp = p.shape[1]
    return pl.pallas_call(
        _ple_kernel,
        grid=(m // PLE_ROW_TILE, n // COL_TILE),
        in_specs=[pl.BlockSpec((PLE_ROW_TILE, k), lambda i, j: (i, 0)),
                  pl.BlockSpec((k, COL_TILE), lambda i, j: (0, j)),
                  pl.BlockSpec((PLE_ROW_TILE, 1), lambda i, j: (i, 0)),
                  pl.BlockSpec((PLE_ROW_TILE, kp), lambda i, j: (i, 0)),
                  pl.BlockSpec((kp, COL_TILE), lambda i, j: (0, j)),
                  pl.BlockSpec((PLE_ROW_TILE, COL_TILE), lambda i, j: (i, j)),
                  pl.BlockSpec((1, n), lambda i, j: (0, 0))],
        out_specs=pl.BlockSpec((PLE_ROW_TILE, n), lambda i, j: (i, 0)),
        out_shape=jax.ShapeDtypeStruct((m, n), F32),
        scratch_shapes=[pltpu.VMEM((PLE_ROW_TILE, 1), F32)],
        compiler_params=_params("parallel", "arbitrary"),
        name="ple_gate",
    )(a, w_gate, ssq, p, w_proj, h, final_gain.reshape(1, n).astype(F32))


def kernel(x, p, norm_mix, w_in, w_s, b_s, norm_a_out, norm_b_out, w_out, norm_ffn, w_up,
           w_down, norm_ple, w_ple_gate, w_ple_proj, norm_final):
    batch, seq, d_model = x.shape
    assert w_in.shape[0] == 1, "single layer: the gated-embedding kernel applies the final rmsnorm"
    a_width = norm_a_out.shape[1]
    b_width = norm_b_out.shape[1]
    sgu_heads = a_width // HEAD_DIM
    sb_heads = b_width // HEAD_DIM
    m = batch * seq
    scale = LOG2E / math.sqrt(HEAD_DIM)
    o1, o2 = a_width, 2 * a_width
    o3, o4 = o2 + b_width, o2 + 2 * b_width

    h = x.reshape(m, d_model)
    a = _rmsnorm(h, norm_mix[0], BF16)
    z, w_up_bf, w_out_bf = _in_proj(a, w_in[0].astype(BF16), o2, o3, scale, w_up[0], w_out[0])
    y_a = _sgu(z, w_s[0], b_s[0], norm_a_out[0], sgu_heads, 0, o1)
    y_b = _attention(z.reshape(batch, seq, -1), norm_b_out[0], sb_heads, o2, o3, o4)
    h, hg, ssq = _out_proj(y_a, y_b.reshape(m, b_width), w_out_bf, h, norm_ffn[0])
    hid, w_down_bf, w_gate_bf = _mlp_up(hg, ssq, w_up_bf, w_down[0], w_ple_gate[0])
    h, hg, ssq = _mlp_down(hid, w_down_bf, h, norm_ple[0])
    out = _ple_final(hg, ssq, w_gate_bf, p[0].reshape(m, PLE_DIM),
                     w_ple_proj[0].astype(BF16), h, norm_final)
    return out.reshape(batch, seq, d_model).astype(x.dtype)
```

```python
import functools
import math

import jax
import jax.numpy as jnp
from jax import lax
from jax.experimental import pallas as pl
from jax.experimental.pallas import tpu as pltpu

F32 = jnp.float32
BF16 = jnp.bfloat16

EPS = 1e-6
HEAD_DIM = 128
CHUNK = 128
PLE_DIM = 256

ROW_TILE = 1024
COL_TILE = 1024
OUT_COL_TILE = 512
PLE_ROW_TILE = 512
DOWN_K_TILE = 2048
RING_AHEAD = 2
DOWN_RING = RING_AHEAD + 1
NORM_ROWS = 256
SGU_ROWS = 2048
ATTN_BLOCK = 256
ATTN_HEADS = 8
ATTN_SKIP_LOG2 = -160.0
MASKED_LOGIT = -1e30
LOG2E = 1.4426950408889634
VMEM_LIMIT = 56 * 1024 * 1024


def _params(*sem):
    return pltpu.CompilerParams(dimension_semantics=sem, vmem_limit_bytes=VMEM_LIMIT)


def _rmsnorm_kernel(x_ref, g_ref, o_ref):
    x = x_ref[...].astype(F32)
    inv = lax.rsqrt(jnp.mean(x * x, axis=-1, keepdims=True) + EPS)
    o_ref[...] = (x * inv * g_ref[...]).astype(o_ref.dtype)


def _rmsnorm(x, g, out_dtype):
    m, d = x.shape
    return pl.pallas_call(
        _rmsnorm_kernel,
        grid=(m // NORM_ROWS,),
        in_specs=[pl.BlockSpec((NORM_ROWS, d), lambda i: (i, 0)),
                  pl.BlockSpec((1, d), lambda i: (0, 0))],
        out_specs=pl.BlockSpec((NORM_ROWS, d), lambda i: (i, 0)),
        out_shape=jax.ShapeDtypeStruct((m, d), out_dtype),
        compiler_params=_params("parallel"),
        name="rmsnorm",
    )(x, g.reshape(1, d).astype(F32))


def _in_proj_kernel(a_ref, w_ref, c0_ref, c1_ref, o_ref, c0_out_ref, c1_out_ref, *,
                    q_lo, q_hi, scale):
    j = pl.program_id(1)
    _cast_rider(c0_ref, c0_out_ref)
    _cast_rider(c1_ref, c1_out_ref)
    acc = jnp.dot(a_ref[...], w_ref[...], preferred_element_type=F32)
    s = jnp.where(jnp.logical_and(j >= q_lo, j < q_hi), scale, 1.0).astype(F32)
    o_ref[...] = (acc * s).astype(o_ref.dtype)


def _cast_rider(src_ref, dst_ref):
    dst_ref[...] = src_ref[...].astype(dst_ref.dtype)


def _rider_spec(w, n_i, j_lo, j_hi):
    rows, cols = w.shape[0] // n_i, w.shape[1] // (j_hi - j_lo)
    assert rows * n_i == w.shape[0] and cols * (j_hi - j_lo) == w.shape[1]
    return pl.BlockSpec((rows, cols), lambda i, j: (i, jnp.clip(j - j_lo, 0, j_hi - j_lo - 1)))


def _in_proj(a, w, q_col_lo, q_col_hi, scale, cast0, cast1):
    m, k = a.shape
    n = w.shape[1]
    n_i, n_j = m // ROW_TILE, n // COL_TILE
    split = n_j - max(1, n_j // 5)
    kern = functools.partial(_in_proj_kernel, q_lo=q_col_lo // COL_TILE,
                             q_hi=q_col_hi // COL_TILE, scale=scale)
    riders = [_rider_spec(cast0, n_i, 0, split), _rider_spec(cast1, n_i, split, n_j)]
    return pl.pallas_call(
        kern,
        grid=(n_i, n_j),
        in_specs=[pl.BlockSpec((ROW_TILE, k), lambda i, j: (i, 0)),
                  pl.BlockSpec((k, COL_TILE), lambda i, j: (0, j))] + riders,
        out_specs=[pl.BlockSpec((ROW_TILE, COL_TILE), lambda i, j: (i, j))] + riders,
        out_shape=(jax.ShapeDtypeStruct((m, n), BF16),
                   jax.ShapeDtypeStruct(cast0.shape, BF16),
                   jax.ShapeDtypeStruct(cast1.shape, BF16)),
        compiler_params=_params("parallel", "arbitrary"),
        name="in_proj",
    )(a, w, cast0, cast1)


def _sgu_kernel(u_ref, v_ref, w_ref, b_ref, g_ref, o_ref):
    u = jax.nn.gelu(u_ref[...].astype(F32))
    v = jax.nn.gelu(v_ref[...].astype(F32))
    mu = jnp.mean(v, axis=-1, keepdims=True)
    vc = v - mu
    var = jnp.mean(vc * vc, axis=-1, keepdims=True)
    vn = (vc * lax.rsqrt(var + EPS)).astype(BF16)
    row = lax.broadcasted_iota(jnp.int32, (CHUNK, CHUNK), 0)
    col = lax.broadcasted_iota(jnp.int32, (CHUNK, CHUNK), 1)
    w = jnp.where(row >= col, w_ref[0], 0.0).astype(BF16)
    bias = b_ref[0]
    g = g_ref[0]
    for c in range(u.shape[0] // CHUNK):
        sl = slice(c * CHUNK, (c + 1) * CHUNK)
        mixed = jnp.dot(w, vn[sl], preferred_element_type=F32) + bias
        y = u[sl] * mixed
        inv = lax.rsqrt(jnp.mean(y * y, axis=-1, keepdims=True) + EPS)
        o_ref[sl, :] = (y * inv * g).astype(o_ref.dtype)


def _sgu(z, w_s, b_s, gain, n_heads, u_col0, v_col0):
    m = z.shape[0]
    ub, vb = u_col0 // HEAD_DIM, v_col0 // HEAD_DIM
    bias = jnp.broadcast_to(b_s.astype(F32)[:, :, None], (n_heads, CHUNK, HEAD_DIM))
    return pl.pallas_call(
        _sgu_kernel,
        grid=(m // SGU_ROWS, n_heads),
        in_specs=[pl.BlockSpec((SGU_ROWS, HEAD_DIM), lambda i, h: (i, ub + h)),
                  pl.BlockSpec((SGU_ROWS, HEAD_DIM), lambda i, h: (i, vb + h)),
                  pl.BlockSpec((1, CHUNK, CHUNK), lambda i, h: (h, 0, 0)),
                  pl.BlockSpec((1, CHUNK, HEAD_DIM), lambda i, h: (h, 0, 0)),
                  pl.BlockSpec((1, 1, HEAD_DIM), lambda i, h: (h, 0, 0))],
        out_specs=pl.BlockSpec((SGU_ROWS, HEAD_DIM), lambda i, h: (i, h)),
        out_shape=jax.ShapeDtypeStruct((m, n_heads * HEAD_DIM), BF16),
        compiler_params=_params("parallel", "parallel"),
        name="sgu",
    )(z, z, w_s.astype(F32), bias, gain.reshape(n_heads, 1, HEAD_DIM).astype(F32))


def _attn_kernel(q_ref, k_ref, v_ref, g_ref, o_ref, acc_ref, carry_ref):
    blk = ATTN_BLOCK
    heads = range(q_ref.shape[2] // HEAD_DIM)
    qi = pl.program_id(2)
    row = lax.broadcasted_iota(jnp.int32, (blk, blk), 0)
    col = lax.broadcasted_iota(jnp.int32, (blk, blk), 1)
    after = jnp.where(row > col, 1.0, 0.0).astype(BF16)
    causal = col < row

    def lanes(h):
        return slice(h * HEAD_DIM, (h + 1) * HEAD_DIM)

    def scores(h, kb):
        k = k_ref[0, pl.ds(kb * blk, blk), lanes(h)]
        return lax.dot_general(q_ref[0, :, lanes(h)], k, (((1,), (1,)), ((), ())),
                               preferred_element_type=F32)

    def log_terms(z, masked):
        if masked:
            z = jnp.where(causal, z, MASKED_LOGIT)
        t = jnp.log(1.0 + jnp.exp2(-jnp.abs(z))) * LOG2E
        log_beta = jnp.minimum(z, 0.0) - t
        return log_beta, log_beta - z

    def attend(tiles, first):
        z = [[scores(h, kb) for kb, _ in tiles] for h in heads]
        logs = [[log_terms(z[h][t], tiles[t][1]) for t in range(len(tiles))] for h in heads]
        sums = [[jnp.dot(lg[1].astype(BF16), after, preferred_element_type=F32) for lg in logs[h]]
                for h in heads]
        weights = []
        for h in heads:
            carry = None if first else carry_ref[h]
            row_w = []
            for t in range(len(tiles)):
                log_beta, log_keep = logs[h][t]
                log_after = sums[h][t] if carry is None else sums[h][t] + carry
                row_w.append(jnp.exp2(log_beta + log_after).astype(BF16))
                block_sum = jnp.sum(log_keep, axis=-1, keepdims=True)
                carry = block_sum if carry is None else carry + block_sum
            carry_ref[h] = carry
            weights.append(row_w)
        for h in heads:
            out = None
            for t, (kb, _) in enumerate(tiles):
                v = v_ref[0, pl.ds(kb * blk, blk), lanes(h)]
                part = jnp.dot(weights[h][t], v, preferred_element_type=F32)
                out = part if out is None else out + part
            acc_ref[h] = out if first else acc_ref[h] + out

    def live():
        top = carry_ref[0]
        for h in heads[1:]:
            top = jnp.maximum(top, carry_ref[h])
        return (jnp.max(top) > ATTN_SKIP_LOG2).astype(jnp.int32)

    @pl.when(qi == 0)
    def _():
        attend([(qi, True)], True)

    @pl.when(qi > 0)
    def _():
        attend([(qi, True), (qi - 1, False)], True)

        def cond(state):
            return jnp.logical_and(state[0] >= 0, state[1] > 0)

        def body(state):
            attend([(state[0], False)], False)
            return state[0] - 1, live()

        lax.while_loop(cond, body, (qi - 2, live()))

    for h in heads:
        acc = acc_ref[h]
        inv = lax.rsqrt(jnp.mean(acc * acc, axis=-1, keepdims=True) + EPS)
        o_ref[0, :, lanes(h)] = (acc * inv * g_ref[0, :, lanes(h)]).astype(o_ref.dtype)


def _attention(z, gain, n_heads, q_col0, k_col0, v_col0):
    b, s, _ = z.shape
    width = ATTN_HEADS * HEAD_DIM
    qb, kb, vb = q_col0 // width, k_col0 // width, v_col0 // width
    return pl.pallas_call(
        _attn_kernel,
        grid=(b, n_heads // ATTN_HEADS, s // ATTN_BLOCK),
        in_specs=[pl.BlockSpec((1, ATTN_BLOCK, width), lambda bi, h, qi: (bi, qi, qb + h)),
                  pl.BlockSpec((1, s, width), lambda bi, h, qi: (bi, 0, kb + h)),
                  pl.BlockSpec((1, s, width), lambda bi, h, qi: (bi, 0, vb + h)),
                  pl.BlockSpec((1, 1, width), lambda bi, h, qi: (0, 0, h))],
        out_specs=pl.BlockSpec((1, ATTN_BLOCK, width), lambda bi, h, qi: (bi, qi, h)),
        out_shape=jax.ShapeDtypeStruct((b, s, n_heads * HEAD_DIM), BF16),
        scratch_shapes=[pltpu.VMEM((ATTN_HEADS, ATTN_BLOCK, HEAD_DIM), F32),
                        pltpu.VMEM((ATTN_HEADS, ATTN_BLOCK, 1), F32)],
        compiler_params=_params("parallel", "parallel", "arbitrary"),
        name="stick_breaking",
    )(z, z, z, gain.reshape(1, 1, n_heads * HEAD_DIM).astype(F32))


def _store_residual(h, first_col_tile, gain_ref, h_ref, hg_ref, ssq_ref):
    h_ref[...] = h
    hg_ref[...] = (h * gain_ref[...]).astype(hg_ref.dtype)

    @pl.when(first_col_tile)
    def _():
        ssq_ref[...] = jnp.zeros_like(ssq_ref)

    ssq_ref[...] += jnp.sum(h * h, axis=-1, keepdims=True)


def _row_inv_rms(ssq_ref, width):
    return lax.rsqrt(ssq_ref[...] * (1.0 / width) + EPS)


def _residual_out(m, n):
    return (jax.ShapeDtypeStruct((m, n), F32), jax.ShapeDtypeStruct((m, n), BF16),
            jax.ShapeDtypeStruct((m, 1), F32))


def _out_proj_kernel(ya_ref, yb_ref, w_ref, x_ref, gain_ref, h_ref, hg_ref, ssq_ref):
    ka = ya_ref.shape[1]
    acc = jnp.dot(ya_ref[...], w_ref[:ka, :], preferred_element_type=F32)
    acc = acc + jnp.dot(yb_ref[...], w_ref[ka:, :], preferred_element_type=F32)
    _store_residual(x_ref[...] + acc, pl.program_id(1) == 0, gain_ref, h_ref, hg_ref, ssq_ref)


def _out_proj(ya, yb, w, x, next_gain):
    m, ka = ya.shape
    kb = yb.shape[1]
    n = w.shape[1]
    tile = pl.BlockSpec((ROW_TILE, OUT_COL_TILE), lambda i, j: (i, j))
    return pl.pallas_call(
        _out_proj_kernel,
        grid=(m // ROW_TILE, n // OUT_COL_TILE),
        in_specs=[pl.BlockSpec((ROW_TILE, ka), lambda i, j: (i, 0)),
                  pl.BlockSpec((ROW_TILE, kb), lambda i, j: (i, 0)),
                  pl.BlockSpec((ka + kb, OUT_COL_TILE), lambda i, j: (0, j)),
                  tile,
                  pl.BlockSpec((1, OUT_COL_TILE), lambda i, j: (0, j))],
        out_specs=(tile, tile, pl.BlockSpec((ROW_TILE, 1), lambda i, j: (i, 0))),
        out_shape=_residual_out(m, n),
        compiler_params=_params("parallel", "arbitrary"),
        name="out_proj",
    )(ya, yb, w, x, next_gain.reshape(1, n).astype(F32))


def _mlp_up_kernel(a_ref, w_ref, ssq_ref, c0_ref, c1_ref, o_ref, c0_out_ref, c1_out_ref):
    _cast_rider(c0_ref, c0_out_ref)
    _cast_rider(c1_ref, c1_out_ref)
    acc = jnp.dot(a_ref[...], w_ref[...], preferred_element_type=F32)
    pre = acc * _row_inv_rms(ssq_ref, a_ref.shape[1])
    o_ref[...] = jnp.square(jnp.maximum(pre, 0.0)).astype(o_ref.dtype)


def _mlp_up(a, ssq, w, cast0, cast1):
    m, k = a.shape
    n = w.shape[1]
    n_i, n_j = m // ROW_TILE, n // COL_TILE
    split = n_j // 2
    riders = [_rider_spec(cast0, n_i, 0, split), _rider_spec(cast1, n_i, split, n_j)]
    return pl.pallas_call(
        _mlp_up_kernel,
        grid=(n_i, n_j),
        in_specs=[pl.BlockSpec((ROW_TILE, k), lambda i, j: (i, 0)),
                  pl.BlockSpec((k, COL_TILE), lambda i, j: (0, j)),
                  pl.BlockSpec((ROW_TILE, 1), lambda i, j: (i, 0))] + riders,
        out_specs=[pl.BlockSpec((ROW_TILE, COL_TILE), lambda i, j: (i, j))] + riders,
        out_shape=(jax.ShapeDtypeStruct((m, n), BF16),
                   jax.ShapeDtypeStruct(cast0.shape, BF16),
                   jax.ShapeDtypeStruct(cast1.shape, BF16)),
        compiler_params=_params("parallel", "arbitrary"),
        name="mlp_up",
    )(a, w, ssq, cast0, cast1)


def _mlp_down_kernel(a_hbm, w_hbm, r_ref, gain_ref, h_ref, hg_ref, ssq_ref,
                     a_ring, w_ring, sems, acc_ref, *, n_i, n_j, n_k):
    kk = pl.program_id(2)
    last = n_k - 1
    step = (pl.program_id(0) * n_j + pl.program_id(1)) * n_k + kk
    n_steps = n_i * n_j * n_k

    def tile_copies(s):
        slot = lax.rem(s, DOWN_RING)
        kk_s = lax.rem(s, n_k)
        ij_s = lax.div(s, n_k)
        rows = pl.ds(lax.div(ij_s, n_j) * ROW_TILE, ROW_TILE)
        cols = pl.ds(lax.rem(ij_s, n_j) * COL_TILE, COL_TILE)
        depth = pl.ds(kk_s * DOWN_K_TILE, DOWN_K_TILE)
        return (pltpu.make_async_copy(a_hbm.at[rows, depth], a_ring.at[slot], sems.at[0, slot]),
                pltpu.make_async_copy(w_hbm.at[depth, cols], w_ring.at[slot], sems.at[1, slot]))

    def start(s):
        for copy in tile_copies(s):
            copy.start()

    @pl.when(step == 0)
    def _():
        for s in range(RING_AHEAD):
            start(s)

    @pl.when(step + RING_AHEAD < n_steps)
    def _():
        start(step + RING_AHEAD)

    for copy in tile_copies(step):
        copy.wait()
    slot = lax.rem(step, DOWN_RING)

    def partial_product():
        return jnp.dot(a_ring[slot], w_ring[slot], preferred_element_type=F32)

    @pl.when(kk == 0)
    def _():
        acc_ref[...] = partial_product()

    @pl.when(jnp.logical_and(kk > 0, kk < last))
    def _():
        acc_ref[...] += partial_product()

    @pl.when(kk == last)
    def _():
        _store_residual(r_ref[...] + (acc_ref[...] + partial_product()), pl.program_id(1) == 0,
                        gain_ref, h_ref, hg_ref, ssq_ref)


def _mlp_down(a, w, r, next_gain):
    m, k = a.shape
    n = w.shape[1]
    n_i, n_j, n_k = m // ROW_TILE, n // COL_TILE, k // DOWN_K_TILE
    assert n_k >= 2, "first and last contraction steps must be distinct"
    assert n_i * n_j * n_k >= RING_AHEAD
    tile = pl.BlockSpec((ROW_TILE, COL_TILE), lambda i, j, kk: (i, j))
    return pl.pallas_call(
        functools.partial(_mlp_down_kernel, n_i=n_i, n_j=n_j, n_k=n_k),
        grid=(n_i, n_j, n_k),
        in_specs=[pl.BlockSpec(memory_space=pl.ANY),
                  pl.BlockSpec(memory_space=pl.ANY),
                  tile,
                  pl.BlockSpec((1, COL_TILE), lambda i, j, kk: (0, j))],
        out_specs=(tile, tile, pl.BlockSpec((ROW_TILE, 1), lambda i, j, kk: (i, 0))),
        out_shape=_residual_out(m, n),
        scratch_shapes=[pltpu.VMEM((DOWN_RING, ROW_TILE, DOWN_K_TILE), BF16),
                        pltpu.VMEM((DOWN_RING, DOWN_K_TILE, COL_TILE), BF16),
                        pltpu.SemaphoreType.DMA((2, DOWN_RING)),
                        pltpu.VMEM((ROW_TILE, COL_TILE), F32)],
        compiler_params=_params("arbitrary", "arbitrary", "arbitrary"),
        name="mlp_down",
    )(a, w, r, next_gain.reshape(1, n).astype(F32))


def _ple_kernel(a_ref, wg_ref, ssq_ref, p_ref, wp_ref, h_ref, gf_ref, o_ref, ssq_out_ref):
    j = pl.program_id(1)
    logits = jnp.dot(a_ref[...], wg_ref[...], preferred_element_type=F32)
    gate = 0.5 * jnp.tanh(logits * (0.5 * _row_inv_rms(ssq_ref, a_ref.shape[1]))) + 0.5
    e = jnp.dot(p_ref[...].astype(BF16), wp_ref[...], preferred_element_type=F32)
    h = h_ref[...] + gate * e
    bn = h.shape[1]
    o_ref[:, pl.ds(pl.multiple_of(j * bn, bn), bn)] = h

    @pl.when(j == 0)
    def _():
        ssq_out_ref[...] = jnp.zeros_like(ssq_out_ref)

    ssq_out_ref[...] += jnp.sum(h * h, axis=-1, keepdims=True)

    @pl.when(j == pl.num_programs(1) - 1)
    def _():
        inv = _row_inv_rms(ssq_out_ref, o_ref.shape[1])
        for jj in range(o_ref.shape[1] // bn):
            cols = slice(jj * bn, (jj + 1) * bn)
            o_ref[:, cols] = o_ref[:, cols] * inv * gf_ref[:, cols]


def _ple_final(a, ssq, w_gate, p, w_proj, h, final_gain):
    m, k = a.shape
    n = w_gate.shape[1]
    kp = p.shape[1]
    return pl.pallas_call(
        _ple_kernel,
        grid=(m // PLE_ROW_TILE, n // COL_TILE),
        in_specs=[pl.BlockSpec((PLE_ROW_TILE, k), lambda i, j: (i, 0)),
                  pl.BlockSpec((k, COL_TILE), lambda i, j: (0, j)),
                  pl.BlockSpec((PLE_ROW_TILE, 1), lambda i, j: (i, 0)),
                  pl.BlockSpec((PLE_ROW_TILE, kp), lambda i, j: (i, 0)),
                  pl.BlockSpec((kp, COL_TILE), lambda i, j: (0, j)),
                  pl.BlockSpec((PLE_ROW_TILE, COL_TILE), lambda i, j: (i, j)),
                  pl.BlockSpec((1, n), lambda i, j: (0, 0))],
        out_specs=pl.BlockSpec((PLE_ROW_TILE, n), lambda i, j: (i, 0)),
        out_shape=jax.ShapeDtypeStruct((m, n), F32),
        scratch_shapes=[pltpu.VMEM((PLE_ROW_TILE, 1), F32)],
        compiler_params=_params("parallel", "arbitrary"),
        name="ple_gate",
    )(a, w_gate, ssq, p, w_proj, h, final_gain.reshape(1, n).astype(F32))


def kernel(x, p, norm_mix, w_in, w_s, b_s, norm_a_out, norm_b_out, w_out, norm_ffn, w_up,
           w_down, norm_ple, w_ple_gate, w_ple_proj, norm_final):
    batch, seq, d_model = x.shape
    assert w_in.shape[0] == 1, "single layer: the gated-embedding kernel applies the final rmsnorm"
    a_width = norm_a_out.shape[1]
    b_width = norm_b_out.shape[1]
    sgu_heads = a_width // HEAD_DIM
    sb_heads = b_width // HEAD_DIM
    m = batch * seq
    scale = LOG2E / math.sqrt(HEAD_DIM)
    o1, o2 = a_width, 2 * a_width
    o3, o4 = o2 + b_width, o2 + 2 * b_width

    h = x.reshape(m, d_model)
    a = _rmsnorm(h, norm_mix[0], BF16)
    z, w_up_bf, w_out_bf = _in_proj(a, w_in[0].astype(BF16), o2, o3, scale, w_up[0], w_out[0])
    y_a = _sgu(z, w_s[0], b_s[0], norm_a_out[0], sgu_heads, 0, o1)
    y_b = _attention(z.reshape(batch, seq, -1), norm_b_out[0], sb_heads, o2, o3, o4)
    h, hg, ssq = _out_proj(y_a, y_b.reshape(m, b_width), w_out_bf, h, norm_ffn[0])
    hid, w_down_bf, w_gate_bf = _mlp_up(hg, ssq, w_up_bf, w_down[0], w_ple_gate[0])
    h, hg, ssq = _mlp_down(hid, w_down_bf, h, norm_ple[0])
    out = _ple_final(hg, ssq, w_gate_bf, p[0].reshape(m, PLE_DIM),
                     w_ple_proj[0].astype(BF16), h, norm_final)
    return out.reshape(batch, seq, d_model).astype(x.dtype)
```

```python
import functools
import math

import jax
import jax.numpy as jnp
from jax import lax
from jax.experimental import pallas as pl
from jax.experimental.pallas import tpu as pltpu

F32 = jnp.float32
BF16 = jnp.bfloat16

EPS = 1e-6
HEAD_DIM = 128
CHUNK = 128
PLE_DIM = 256

ROW_TILE = 1024
COL_TILE = 1024
OUT_COL_TILE = 512
PLE_ROW_TILE = 512
DOWN_K_TILE = 2048
RING_AHEAD = 2
DOWN_RING = RING_AHEAD + 1
NORM_ROWS = 256
SGU_ROWS = 2048
ATTN_BLOCK = 256
ATTN_HEADS = 8
ATTN_SKIP_LOG2 = -160.0
MASKED_LOGIT = -1e30
LOG2E = 1.4426950408889634
VMEM_LIMIT = 56 * 1024 * 1024


def _params(*sem):
    return pltpu.CompilerParams(dimension_semantics=sem, vmem_limit_bytes=VMEM_LIMIT)


def _rmsnorm_kernel(x_ref, g_ref, o_ref):
    x = x_ref[...].astype(F32)
    inv = lax.rsqrt(jnp.mean(x * x, axis=-1, keepdims=True) + EPS)
    o_ref[...] = (x * inv * g_ref[...]).astype(o_ref.dtype)


def _rmsnorm(x, g, out_dtype):
    m, d = x.shape
    return pl.pallas_call(
        _rmsnorm_kernel,
        grid=(m // NORM_ROWS,),
        in_specs=[pl.BlockSpec((NORM_ROWS, d), lambda i: (i, 0)),
                  pl.BlockSpec((1, d), lambda i: (0, 0))],
        out_specs=pl.BlockSpec((NORM_ROWS, d), lambda i: (i, 0)),
        out_shape=jax.ShapeDtypeStruct((m, d), out_dtype),
        compiler_params=_params("parallel"),
        name="rmsnorm",
    )(x, g.reshape(1, d).astype(F32))


def _in_proj_kernel(a_ref, w_ref, c0_ref, c1_ref, o_ref, c0_out_ref, c1_out_ref, *,
                    q_lo, q_hi, scale):
    j = pl.program_id(1)
    _cast_rider(c0_ref, c0_out_ref)
    _cast_rider(c1_ref, c1_out_ref)
    acc = jnp.dot(a_ref[...], w_ref[...], preferred_element_type=F32)
    s = jnp.where(jnp.logical_and(j >= q_lo, j < q_hi), scale, 1.0).astype(F32)
    o_ref[...] = (acc * s).astype(o_ref.dtype)


def _cast_rider(src_ref, dst_ref):
    dst_ref[...] = src_ref[...].astype(dst_ref.dtype)


def _rider_spec(w, n_i, j_lo, j_hi):
    rows, cols = w.shape[0] // n_i, w.shape[1] // (j_hi - j_lo)
    assert rows * n_i == w.shape[0] and cols * (j_hi - j_lo) == w.shape[1]
    return pl.BlockSpec((rows, cols), lambda i, j: (i, jnp.clip(j - j_lo, 0, j_hi - j_lo - 1)))


def _in_proj(a, w, q_col_lo, q_col_hi, scale, cast0, cast1):
    m, k = a.shape
    n = w.shape[1]
    n_i, n_j = m // ROW_TILE, n // COL_TILE
    split = n_j - max(1, n_j // 5)
    kern = functools.partial(_in_proj_kernel, q_lo=q_col_lo // COL_TILE,
                             q_hi=q_col_hi // COL_TILE, scale=scale)
    riders = [_rider_spec(cast0, n_i, 0, split), _rider_spec(cast1, n_i, split, n_j)]
    return pl.pallas_call(
        kern,
        grid=(n_i, n_j),
        in_specs=[pl.BlockSpec((ROW_TILE, k), lambda i, j: (i, 0)),
                  pl.BlockSpec((k, COL_TILE), lambda i, j: (0, j))] + riders,
        out_specs=[pl.BlockSpec((ROW_TILE, COL_TILE), lambda i, j: (i, j))] + riders,
        out_shape=(jax.ShapeDtypeStruct((m, n), BF16),
                   jax.ShapeDtypeStruct(cast0.shape, BF16),
                   jax.ShapeDtypeStruct(cast1.shape, BF16)),
        compiler_params=_params("parallel", "arbitrary"),
        name="in_proj",
    )(a, w, cast0, cast1)


def _sgu_kernel(u_ref, v_ref, w_ref, b_ref, g_ref, o_ref):
    u = jax.nn.gelu(u_ref[...].astype(F32))
    v = jax.nn.gelu(v_ref[...].astype(F32))
    mu = jnp.mean(v, axis=-1, keepdims=True)
    vc = v - mu
    var = jnp.mean(vc * vc, axis=-1, keepdims=True)
    vn = (vc * lax.rsqrt(var + EPS)).astype(BF16)
    row = lax.broadcasted_iota(jnp.int32, (CHUNK, CHUNK), 0)
    col = lax.broadcasted_iota(jnp.int32, (CHUNK, CHUNK), 1)
    w = jnp.where(row >= col, w_ref[0], 0.0).astype(BF16)
    bias = b_ref[0]
    g = g_ref[0]
    for c in range(u.shape[0] // CHUNK):
        sl = slice(c * CHUNK, (c + 1) * CHUNK)
        mixed = jnp.dot(w, vn[sl], preferred_element_type=F32) + bias
        y = u[sl] * mixed
        inv = lax.rsqrt(jnp.mean(y * y, axis=-1, keepdims=True) + EPS)
        o_ref[sl, :] = (y * inv * g).astype(o_ref.dtype)


def _sgu(z, w_s, b_s, gain, n_heads, u_col0, v_col0):
    m = z.shape[0]
    ub, vb = u_col0 // HEAD_DIM, v_col0 // HEAD_DIM
    bias = jnp.broadcast_to(b_s.astype(F32)[:, :, None], (n_heads, CHUNK, HEAD_DIM))
    return pl.pallas_call(
        _sgu_kernel,
        grid=(m // SGU_ROWS, n_heads),
        in_specs=[pl.BlockSpec((SGU_ROWS, HEAD_DIM), lambda i, h: (i, ub + h)),
                  pl.BlockSpec((SGU_ROWS, HEAD_DIM), lambda i, h: (i, vb + h)),
                  pl.BlockSpec((1, CHUNK, CHUNK), lambda i, h: (h, 0, 0)),
                  pl.BlockSpec((1, CHUNK, HEAD_DIM), lambda i, h: (h, 0, 0)),
                  pl.BlockSpec((1, 1, HEAD_DIM), lambda i, h: (h, 0, 0))],
        out_specs=pl.BlockSpec((SGU_ROWS, HEAD_DIM), lambda i, h: (i, h)),
        out_shape=jax.ShapeDtypeStruct((m, n_heads * HEAD_DIM), BF16),
        compiler_params=_params("parallel", "parallel"),
        name="sgu",
    )(z, z, w_s.astype(F32), bias, gain.reshape(n_heads, 1, HEAD_DIM).astype(F32))


def _attn_kernel(q_ref, k_ref, v_ref, g_ref, o_ref, acc_ref, carry_ref):
    blk = ATTN_BLOCK
    heads = range(q_ref.shape[2] // HEAD_DIM)
    qi = pl.program_id(2)
    row = lax.broadcasted_iota(jnp.int32, (blk, blk), 0)
    col = lax.broadcasted_iota(jnp.int32, (blk, blk), 1)
    after = jnp.where(row > col, 1.0, 0.0).astype(BF16)
    causal = col < row

    def lanes(h):
        return slice(h * HEAD_DIM, (h + 1) * HEAD_DIM)

    def scores(h, kb):
        k = k_ref[0, pl.ds(kb * blk, blk), lanes(h)]
        return lax.dot_general(q_ref[0, :, lanes(h)], k, (((1,), (1,)), ((), ())),
                               preferred_element_type=F32)

    def log_terms(z, masked):
        if masked:
            z = jnp.where(causal, z, MASKED_LOGIT)
        t = jnp.log(1.0 + jnp.exp2(-jnp.abs(z))) * LOG2E
        log_beta = jnp.minimum(z, 0.0) - t
        return log_beta, log_beta - z

    def attend(tiles, first):
        z = [[scores(h, kb) for kb, _ in tiles] for h in heads]
        logs = [[log_terms(z[h][t], tiles[t][1]) for t in range(len(tiles))] for h in heads]
        sums = [[jnp.dot(lg[1].astype(BF16), after, preferred_element_type=F32) for lg in logs[h]]
                for h in heads]
        weights = []
        for h in heads:
            carry = None if first else carry_ref[h]
            row_w = []
            for t in range(len(tiles)):
                log_beta, log_keep = logs[h][t]
                log_after = sums[h][t] if carry is None else sums[h][t] + carry
                row_w.append(jnp.exp2(log_beta + log_after).astype(BF16))
                block_sum = jnp.sum(log_keep, axis=-1, keepdims=True)
                carry = block_sum if carry is None else carry + block_sum
            carry_ref[h] = carry
            weights.append(row_w)
        for h in heads:
            out = None
            for t, (kb, _) in enumerate(tiles):
                v = v_ref[0, pl.ds(kb * blk, blk), lanes(h)]
                part = jnp.dot(weights[h][t], v, preferred_element_type=F32)
                out = part if out is None else out + part
            acc_ref[h] = out if first else acc_ref[h] + out

    def live():
        top = carry_ref[0]
        for h in heads[1:]:
            top = jnp.maximum(top, carry_ref[h])
        return (jnp.max(top) > ATTN_SKIP_LOG2).astype(jnp.int32)

    @pl.when(qi == 0)
    def _():
        attend([(qi, True)], True)

    @pl.when(qi > 0)
    def _():
        attend([(qi, True), (qi - 1, False)], True)

        def cond(state):
            return jnp.logical_and(state[0] >= 0, state[1] > 0)

        def body(state):
            attend([(state[0], False)], False)
            return state[0] - 1, live()

        lax.while_loop(cond, body, (qi - 2, live()))

    for h in heads:
        acc = acc_ref[h]
        inv = lax.rsqrt(jnp.mean(acc * acc, axis=-1, keepdims=True) + EPS)
        o_ref[0, :, lanes(h)] = (acc * inv * g_ref[0, :, lanes(h)]).astype(o_ref.dtype)


def _attention(z, gain, n_heads, q_col0, k_col0, v_col0):
    b, s, _ = z.shape
    width = ATTN_HEADS * HEAD_DIM
    qb, kb, vb = q_col0 // width, k_col0 // width, v_col0 // width
    return pl.pallas_call(
        _attn_kernel,
        grid=(b, n_heads // ATTN_HEADS, s // ATTN_BLOCK),
        in_specs=[pl.BlockSpec((1, ATTN_BLOCK, width), lambda bi, h, qi: (bi, qi, qb + h)),
                  pl.BlockSpec((1, s, width), lambda bi, h, qi: (bi, 0, kb + h)),
                  pl.BlockSpec((1, s, width), lambda bi, h, qi: (bi, 0, vb + h)),
                  pl.BlockSpec((1, 1, width), lambda bi, h, qi: (0, 0, h))],
        out_specs=pl.BlockSpec((1, ATTN_BLOCK, width), lambda bi, h, qi: (bi, qi, h)),
        out_shape=jax.ShapeDtypeStruct((b, s, n_heads * HEAD_DIM), BF16),
        scratch_shapes=[pltpu.VMEM((ATTN_HEADS, ATTN_BLOCK, HEAD_DIM), F32),
                        pltpu.VMEM((ATTN_HEADS, ATTN_BLOCK, 1), F32)],
        compiler_params=_params("parallel", "parallel", "arbitrary"),
        name="stick_breaking",
    )(z, z, z, gain.reshape(1, 1, n_heads * HEAD_DIM).astype(F32))


def _store_residual(h, first_col_tile, gain_ref, h_ref, hg_ref, ssq_ref):
    h_ref[...] = h
    hg_ref[...] = (h * gain_ref[...]).astype(hg_ref.dtype)

    @pl.when(first_col_tile)
    def _():
        ssq_ref[...] = jnp.zeros_like(ssq_ref)

    ssq_ref[...] += jnp.sum(h * h, axis=-1, keepdims=True)


def _snake(i, j, n_j):
    return jnp.where((i & 1) == 0, j, n_j - 1 - j)


def _row_inv_rms(ssq_ref, width):
    return lax.rsqrt(ssq_ref[...] * (1.0 / width) + EPS)


def _residual_out(m, n):
    return (jax.ShapeDtypeStruct((m, n), F32), jax.ShapeDtypeStruct((m, n), BF16),
            jax.ShapeDtypeStruct((m, 1), F32))


def _out_proj_kernel(ya_ref, yb_ref, w_ref, x_ref, gain_ref, h_ref, hg_ref, ssq_ref):
    ka = ya_ref.shape[1]
    acc = jnp.dot(ya_ref[...], w_ref[:ka, :], preferred_element_type=F32)
    acc = acc + jnp.dot(yb_ref[...], w_ref[ka:, :], preferred_element_type=F32)
    _store_residual(x_ref[...] + acc, pl.program_id(1) == 0, gain_ref, h_ref, hg_ref, ssq_ref)


def _out_proj(ya, yb, w, x, next_gain):
    m, ka = ya.shape
    kb = yb.shape[1]
    n = w.shape[1]
    n_j = n // OUT_COL_TILE
    tile = pl.BlockSpec((ROW_TILE, OUT_COL_TILE), lambda i, j: (i, _snake(i, j, n_j)))
    return pl.pallas_call(
        _out_proj_kernel,
        grid=(m // ROW_TILE, n_j),
        in_specs=[pl.BlockSpec((ROW_TILE, ka), lambda i, j: (i, 0)),
                  pl.BlockSpec((ROW_TILE, kb), lambda i, j: (i, 0)),
                  pl.BlockSpec((ka + kb, OUT_COL_TILE), lambda i, j: (0, _snake(i, j, n_j))),
                  tile,
                  pl.BlockSpec((1, OUT_COL_TILE), lambda i, j: (0, _snake(i, j, n_j)))],
        out_specs=(tile, tile, pl.BlockSpec((ROW_TILE, 1), lambda i, j: (i, 0))),
        out_shape=_residual_out(m, n),
        compiler_params=_params("arbitrary", "arbitrary"),
        name="out_proj",
    )(ya, yb, w, x, next_gain.reshape(1, n).astype(F32))


def _mlp_up_kernel(a_ref, w_ref, ssq_ref, c0_ref, c1_ref, o_ref, c0_out_ref, c1_out_ref):
    _cast_rider(c0_ref, c0_out_ref)
    _cast_rider(c1_ref, c1_out_ref)
    acc = jnp.dot(a_ref[...], w_ref[...], preferred_element_type=F32)
    pre = acc * _row_inv_rms(ssq_ref, a_ref.shape[1])
    o_ref[...] = jnp.square(jnp.maximum(pre, 0.0)).astype(o_ref.dtype)


def _mlp_up(a, ssq, w, cast0, cast1):
    m, k = a.shape
    n = w.shape[1]
    n_i, n_j = m // ROW_TILE, n // COL_TILE
    split = n_j // 2
    riders = [_rider_spec(cast0, n_i, 0, split), _rider_spec(cast1, n_i, split, n_j)]
    return pl.pallas_call(
        _mlp_up_kernel,
        grid=(n_i, n_j),
        in_specs=[pl.BlockSpec((ROW_TILE, k), lambda i, j: (i, 0)),
                  pl.BlockSpec((k, COL_TILE), lambda i, j: (0, j)),
                  pl.BlockSpec((ROW_TILE, 1), lambda i, j: (i, 0))] + riders,
        out_specs=[pl.BlockSpec((ROW_TILE, COL_TILE), lambda i, j: (i, j))] + riders,
        out_shape=(jax.ShapeDtypeStruct((m, n), BF16),
                   jax.ShapeDtypeStruct(cast0.shape, BF16),
                   jax.ShapeDtypeStruct(cast1.shape, BF16)),
        compiler_params=_params("parallel", "arbitrary"),
        name="mlp_up",
    )(a, w, ssq, cast0, cast1)


def _mlp_down_kernel(a_hbm, w_hbm, r_ref, gain_ref, h_ref, hg_ref, ssq_ref,
                     a_ring, w_ring, sems, acc_ref, *, n_i, n_j, n_k):
    kk = pl.program_id(2)
    last = n_k - 1
    step = (pl.program_id(0) * n_j + pl.program_id(1)) * n_k + kk
    n_steps = n_i * n_j * n_k

    def tile_copies(s):
        slot = lax.rem(s, DOWN_RING)
        kk_s = lax.rem(s, n_k)
        ij_s = lax.div(s, n_k)
        rows = pl.ds(lax.div(ij_s, n_j) * ROW_TILE, ROW_TILE)
        cols = pl.ds(lax.rem(ij_s, n_j) * COL_TILE, COL_TILE)
        depth = pl.ds(kk_s * DOWN_K_TILE, DOWN_K_TILE)
        return (pltpu.make_async_copy(a_hbm.at[rows, depth], a_ring.at[slot], sems.at[0, slot]),
                pltpu.make_async_copy(w_hbm.at[depth, cols], w_ring.at[slot], sems.at[1, slot]))

    def start(s):
        for copy in tile_copies(s):
            copy.start()

    @pl.when(step == 0)
    def _():
        for s in range(RING_AHEAD):
            start(s)

    @pl.when(step + RING_AHEAD < n_steps)
    def _():
        start(step + RING_AHEAD)

    for copy in tile_copies(step):
        copy.wait()
    slot = lax.rem(step, DOWN_RING)

    def partial_product():
        return jnp.dot(a_ring[slot], w_ring[slot], preferred_element_type=F32)

    @pl.when(kk == 0)
    def _():
        acc_ref[...] = partial_product()

    @pl.when(jnp.logical_and(kk > 0, kk < last))
    def _():
        acc_ref[...] += partial_product()

    @pl.when(kk == last)
    def _():
        _store_residual(r_ref[...] + (acc_ref[...] + partial_product()), pl.program_id(1) == 0,
                        gain_ref, h_ref, hg_ref, ssq_ref)


def _mlp_down(a, w, r, next_gain):
    m, k = a.shape
    n = w.shape[1]
    n_i, n_j, n_k = m // ROW_TILE, n // COL_TILE, k // DOWN_K_TILE
    assert n_k >= 2, "first and last contraction steps must be distinct"
    assert n_i * n_j * n_k >= RING_AHEAD
    tile = pl.BlockSpec((ROW_TILE, COL_TILE), lambda i, j, kk: (i, j))
    return pl.pallas_call(
        functools.partial(_mlp_down_kernel, n_i=n_i, n_j=n_j, n_k=n_k),
        grid=(n_i, n_j, n_k),
        in_specs=[pl.BlockSpec(memory_space=pl.ANY),
                  pl.BlockSpec(memory_space=pl.ANY),
                  tile,
                  pl.BlockSpec((1, COL_TILE), lambda i, j, kk: (0, j))],
        out_specs=(tile, tile, pl.BlockSpec((ROW_TILE, 1), lambda i, j, kk: (i, 0))),
        out_shape=_residual_out(m, n),
        scratch_shapes=[pltpu.VMEM((DOWN_RING, ROW_TILE, DOWN_K_TILE), BF16),
                        pltpu.VMEM((DOWN_RING, DOWN_K_TILE, COL_TILE), BF16),
                        pltpu.SemaphoreType.DMA((2, DOWN_RING)),
                        pltpu.VMEM((ROW_TILE, COL_TILE), F32)],
        compiler_params=_params("arbitrary", "arbitrary", "arbitrary"),
        name="mlp_down",
    )(a, w, r, next_gain.reshape(1, n).astype(F32))


def _ple_kernel(a_ref, wg_ref, ssq_ref, p_ref, wp_ref, h_ref, gf_ref, o_ref, ssq_out_ref):
    j = pl.program_id(1)
    logits = jnp.dot(a_ref[...], wg_ref[...], preferred_element_type=F32)
    gate = 0.5 * jnp.tanh(logits * (0.5 * _row_inv_rms(ssq_ref, a_ref.shape[1]))) + 0.5
    e = jnp.dot(p_ref[...].astype(BF16), wp_ref[...], preferred_element_type=F32)
    h = h_ref[...] + gate * e
    bn = h.shape[1]
    col = _snake(pl.program_id(0), j, pl.num_programs(1))
    o_ref[:, pl.ds(pl.multiple_of(col * bn, bn), bn)] = h

    @pl.when(j == 0)
    def _():
        ssq_out_ref[...] = jnp.zeros_like(ssq_out_ref)

    ssq_out_ref[...] += jnp.sum(h * h, axis=-1, keepdims=True)

    @pl.when(j == pl.num_programs(1) - 1)
    def _():
        inv = _row_inv_rms(ssq_out_ref, o_ref.shape[1])
        for jj in range(o_ref.shape[1] // bn):
            cols = slice(jj * bn, (jj + 1) * bn)
            o_ref[:, cols] = o_ref[:, cols] * inv * gf_ref[:, cols]


def _ple_final(a, ssq, w_gate, p, w_proj, h, final_gain):
    m, k = a.shape
    n = w_gate.shape[1]
    kp = p.shape[1]
    n_j = n // COL_TILE
    return pl.pallas_call(
        _ple_kernel,
        grid=(m // PLE_ROW_TILE, n_j),
        in_specs=[pl.BlockSpec((PLE_ROW_TILE, k), lambda i, j: (i, 0)),
                  pl.BlockSpec((k, COL_TILE), lambda i, j: (0, _snake(i, j, n_j))),
                  pl.BlockSpec((PLE_ROW_TILE, 1), lambda i, j: (i, 0)),
                  pl.BlockSpec((PLE_ROW_TILE, kp), lambda i, j: (i, 0)),
                  pl.BlockSpec((kp, COL_TILE), lambda i, j: (0, _snake(i, j, n_j))),
                  pl.BlockSpec((PLE_ROW_TILE, COL_TILE), lambda i, j: (i, _snake(i, j, n_j))),
                  pl.BlockSpec((1, n), lambda i, j: (0, 0))],
        out_specs=pl.BlockSpec((PLE_ROW_TILE, n), lambda i, j: (i, 0)),
        out_shape=jax.ShapeDtypeStruct((m, n), F32),
        scratch_shapes=[pltpu.VMEM((PLE_ROW_TILE, 1), F32)],
        compiler_params=_params("arbitrary", "arbitrary"),
        name="ple_gate",
    )(a, w_gate, ssq, p, w_proj, h, final_gain.reshape(1, n).astype(F32))


def kernel(x, p, norm_mix, w_in, w_s, b_s, norm_a_out, norm_b_out, w_out, norm_ffn, w_up,
           w_down, norm_ple, w_ple_gate, w_ple_proj, norm_final):
    batch, seq, d_model = x.shape
    assert w_in.shape[0] == 1, "single layer: the gated-embedding kernel applies the final rmsnorm"
    a_width = norm_a_out.shape[1]
    b_width = norm_b_out.shape[1]
    sgu_heads = a_width // HEAD_DIM
    sb_heads = b_width // HEAD_DIM
    m = batch * seq
    scale = LOG2E / math.sqrt(HEAD_DIM)
    o1, o2 = a_width, 2 * a_width
    o3, o4 = o2 + b_width, o2 + 2 * b_width

    h = x.reshape(m, d_model)
    a = _rmsnorm(h, norm_mix[0], BF16)
    z, w_up_bf, w_out_bf = _in_proj(a, w_in[0].astype(BF16), o2, o3, scale, w_up[0], w_out[0])
    y_a = _sgu(z, w_s[0], b_s[0], norm_a_out[0], sgu_heads, 0, o1)
    y_b = _attention(z.reshape(batch, seq, -1), norm_b_out[0], sb_heads, o2, o3, o4)
    h, hg, ssq = _out_proj(y_a, y_b.reshape(m, b_width), w_out_bf, h, norm_ffn[0])
    hid, w_down_bf, w_gate_bf = _mlp_up(hg, ssq, w_up_bf, w_down[0], w_ple_gate[0])
    h, hg, ssq = _mlp_down(hid, w_down_bf, h, norm_ple[0])
    out = _ple_final(hg, ssq, w_gate_bf, p[0].reshape(m, PLE_DIM),
                     w_ple_proj[0].astype(BF16), h, norm_final)
    return out.reshape(batch, seq, d_model).astype(x.dtype)
```

```python
import functools
import math

import jax
import jax.numpy as jnp
from jax import lax
from jax.experimental import pallas as pl
from jax.experimental.pallas import tpu as pltpu

F32 = jnp.float32
BF16 = jnp.bfloat16

EPS = 1e-6
HEAD_DIM = 128
CHUNK = 128
PLE_DIM = 256

ROW_TILE = 1024
COL_TILE = 1024
OUT_COL_TILE = 512
PLE_ROW_TILE = 512
DOWN_K_TILE = 2048
RING_AHEAD = 2
DOWN_RING = RING_AHEAD + 1
NORM_ROWS = 256
SGU_ROWS = 2048
ATTN_BLOCK = 256
ATTN_HEADS = 8
ATTN_SKIP_LOG2 = -160.0
MASKED_LOGIT = -1e30
LOG2E = 1.4426950408889634
VMEM_LIMIT = 56 * 1024 * 1024


def _params(*sem):
    return pltpu.CompilerParams(dimension_semantics=sem, vmem_limit_bytes=VMEM_LIMIT)


def _rmsnorm_kernel(x_ref, g_ref, o_ref):
    x = x_ref[...].astype(F32)
    inv = lax.rsqrt(jnp.mean(x * x, axis=-1, keepdims=True) + EPS)
    o_ref[...] = (x * inv * g_ref[...]).astype(o_ref.dtype)


def _rmsnorm(x, g, out_dtype):
    m, d = x.shape
    return pl.pallas_call(
        _rmsnorm_kernel,
        grid=(m // NORM_ROWS,),
        in_specs=[pl.BlockSpec((NORM_ROWS, d), lambda i: (i, 0)),
                  pl.BlockSpec((1, d), lambda i: (0, 0))],
        out_specs=pl.BlockSpec((NORM_ROWS, d), lambda i: (i, 0)),
        out_shape=jax.ShapeDtypeStruct((m, d), out_dtype),
        compiler_params=_params("parallel"),
        name="rmsnorm",
    )(x, g.reshape(1, d).astype(F32))


def _in_proj_kernel(a_ref, w_ref, c0_ref, c1_ref, o_ref, c0_out_ref, c1_out_ref, *,
                    q_lo, q_hi, scale):
    j = pl.program_id(1)
    _cast_rider(c0_ref, c0_out_ref)
    _cast_rider(c1_ref, c1_out_ref)
    acc = jnp.dot(a_ref[...], w_ref[...], preferred_element_type=F32)
    s = jnp.where(jnp.logical_and(j >= q_lo, j < q_hi), scale, 1.0).astype(F32)
    o_ref[...] = (acc * s).astype(o_ref.dtype)


def _cast_rider(src_ref, dst_ref):
    dst_ref[...] = src_ref[...].astype(dst_ref.dtype)


def _rider_spec(w, n_i, j_lo, j_hi):
    rows, cols = w.shape[0] // n_i, w.shape[1] // (j_hi - j_lo)
    assert rows * n_i == w.shape[0] and cols * (j_hi - j_lo) == w.shape[1]
    return pl.BlockSpec((rows, cols), lambda i, j: (i, jnp.clip(j - j_lo, 0, j_hi - j_lo - 1)))


def _in_proj(a, w, q_col_lo, q_col_hi, scale, cast0, cast1):
    m, k = a.shape
    n = w.shape[1]
    n_i, n_j = m // ROW_TILE, n // COL_TILE
    split = n_j - max(1, n_j // 5)
    kern = functools.partial(_in_proj_kernel, q_lo=q_col_lo // COL_TILE,
                             q_hi=q_col_hi // COL_TILE, scale=scale)
    riders = [_rider_spec(cast0, n_i, 0, split), _rider_spec(cast1, n_i, split, n_j)]
    return pl.pallas_call(
        kern,
        grid=(n_i, n_j),
        in_specs=[pl.BlockSpec((ROW_TILE, k), lambda i, j: (i, 0)),
                  pl.BlockSpec((k, COL_TILE), lambda i, j: (0, j))] + riders,
        out_specs=[pl.BlockSpec((ROW_TILE, COL_TILE), lambda i, j: (i, j))] + riders,
        out_shape=(jax.ShapeDtypeStruct((m, n), BF16),
                   jax.ShapeDtypeStruct(cast0.shape, BF16),
                   jax.ShapeDtypeStruct(cast1.shape, BF16)),
        compiler_params=_params("parallel", "arbitrary"),
        name="in_proj",
    )(a, w, cast0, cast1)


def _sgu_kernel(u_ref, v_ref, w_ref, b_ref, g_ref, o_ref):
    u = jax.nn.gelu(u_ref[...].astype(F32))
    v = jax.nn.gelu(v_ref[...].astype(F32))
    mu = jnp.mean(v, axis=-1, keepdims=True)
    vc = v - mu
    var = jnp.mean(vc * vc, axis=-1, keepdims=True)
    vn = (vc * lax.rsqrt(var + EPS)).astype(BF16)
    row = lax.broadcasted_iota(jnp.int32, (CHUNK, CHUNK), 0)
    col = lax.broadcasted_iota(jnp.int32, (CHUNK, CHUNK), 1)
    w = jnp.where(row >= col, w_ref[0], 0.0).astype(BF16)
    bias = b_ref[0]
    g = g_ref[0]
    for c in range(u.shape[0] // CHUNK):
        sl = slice(c * CHUNK, (c + 1) * CHUNK)
        mixed = jnp.dot(w, vn[sl], preferred_element_type=F32) + bias
        y = u[sl] * mixed
        inv = lax.rsqrt(jnp.mean(y * y, axis=-1, keepdims=True) + EPS)
        o_ref[sl, :] = (y * inv * g).astype(o_ref.dtype)


def _sgu(z, w_s, b_s, gain, n_heads, u_col0, v_col0):
    m = z.shape[0]
    ub, vb = u_col0 // HEAD_DIM, v_col0 // HEAD_DIM
    bias = jnp.broadcast_to(b_s.astype(F32)[:, :, None], (n_heads, CHUNK, HEAD_DIM))
    return pl.pallas_call(
        _sgu_kernel,
        grid=(m // SGU_ROWS, n_heads),
        in_specs=[pl.BlockSpec((SGU_ROWS, HEAD_DIM), lambda i, h: (i, ub + h)),
                  pl.BlockSpec((SGU_ROWS, HEAD_DIM), lambda i, h: (i, vb + h)),
                  pl.BlockSpec((1, CHUNK, CHUNK), lambda i, h: (h, 0, 0)),
                  pl.BlockSpec((1, CHUNK, HEAD_DIM), lambda i, h: (h, 0, 0)),
                  pl.BlockSpec((1, 1, HEAD_DIM), lambda i, h: (h, 0, 0))],
        out_specs=pl.BlockSpec((SGU_ROWS, HEAD_DIM), lambda i, h: (i, h)),
        out_shape=jax.ShapeDtypeStruct((m, n_heads * HEAD_DIM), BF16),
        compiler_params=_params("parallel", "parallel"),
        name="sgu",
    )(z, z, w_s.astype(F32), bias, gain.reshape(n_heads, 1, HEAD_DIM).astype(F32))


def _attn_kernel(q_ref, k_ref, v_ref, g_ref, o_ref, acc_ref, carry_ref):
    blk = ATTN_BLOCK
    heads = range(q_ref.shape[2] // HEAD_DIM)
    qi = pl.program_id(2)
    row = lax.broadcasted_iota(jnp.int32, (blk, blk), 0)
    col = lax.broadcasted_iota(jnp.int32, (blk, blk), 1)
    after = jnp.where(row > col, 1.0, 0.0).astype(BF16)
    causal = col < row

    def lanes(h):
        return slice(h * HEAD_DIM, (h + 1) * HEAD_DIM)

    def scores(h, kb):
        k = k_ref[0, pl.ds(kb * blk, blk), lanes(h)]
        return lax.dot_general(q_ref[0, :, lanes(h)], k, (((1,), (1,)), ((), ())),
                               preferred_element_type=F32)

    def log_terms(z, masked):
        if masked:
            z = jnp.where(causal, z, MASKED_LOGIT)
        t = jnp.log(1.0 + jnp.exp2(-jnp.abs(z))) * LOG2E
        log_beta = jnp.minimum(z, 0.0) - t
        return log_beta, log_beta - z

    def attend(tiles, first):
        z = [[scores(h, kb) for kb, _ in tiles] for h in heads]
        logs = [[log_terms(z[h][t], tiles[t][1]) for t in range(len(tiles))] for h in heads]
        sums = [[jnp.dot(lg[1].astype(BF16), after, preferred_element_type=F32) for lg in logs[h]]
                for h in heads]
        weights = []
        for h in heads:
            carry = None if first else carry_ref[h]
            row_w = []
            for t in range(len(tiles)):
                log_beta, log_keep = logs[h][t]
                log_after = sums[h][t] if carry is None else sums[h][t] + carry
                row_w.append(jnp.exp2(log_beta + log_after).astype(BF16))
                block_sum = jnp.sum(log_keep, axis=-1, keepdims=True)
                carry = block_sum if carry is None else carry + block_sum
            carry_ref[h] = carry
            weights.append(row_w)
        for h in heads:
            out = None
            for t, (kb, _) in enumerate(tiles):
                v = v_ref[0, pl.ds(kb * blk, blk), lanes(h)]
                part = jnp.dot(weights[h][t], v, preferred_element_type=F32)
                out = part if out is None else out + part
            acc_ref[h] = out if first else acc_ref[h] + out

    def live():
        top = carry_ref[0]
        for h in heads[1:]:
            top = jnp.maximum(top, carry_ref[h])
        return (jnp.max(top) > ATTN_SKIP_LOG2).astype(jnp.int32)

    @pl.when(qi == 0)
    def _():
        attend([(qi, True)], True)

    @pl.when(qi > 0)
    def _():
        attend([(qi, True), (qi - 1, False)], True)

        def cond(state):
            return jnp.logical_and(state[0] >= 0, state[1] > 0)

        def body(state):
            attend([(state[0], False)], False)
            return state[0] - 1, live()

        lax.while_loop(cond, body, (qi - 2, live()))

    for h in heads:
        acc = acc_ref[h]
        inv = lax.rsqrt(jnp.mean(acc * acc, axis=-1, keepdims=True) + EPS)
        o_ref[0, :, lanes(h)] = (acc * inv * g_ref[0, :, lanes(h)]).astype(o_ref.dtype)


def _attention(z, gain, n_heads, q_col0, k_col0, v_col0):
    b, s, _ = z.shape
    width = ATTN_HEADS * HEAD_DIM
    qb, kb, vb = q_col0 // width, k_col0 // width, v_col0 // width
    return pl.pallas_call(
        _attn_kernel,
        grid=(b, n_heads // ATTN_HEADS, s // ATTN_BLOCK),
        in_specs=[pl.BlockSpec((1, ATTN_BLOCK, width), lambda bi, h, qi: (bi, qi, qb + h)),
                  pl.BlockSpec((1, s, width), lambda bi, h, qi: (bi, 0, kb + h)),
                  pl.BlockSpec((1, s, width), lambda bi, h, qi: (bi, 0, vb + h)),
                  pl.BlockSpec((1, 1, width), lambda bi, h, qi: (0, 0, h))],
        out_specs=pl.BlockSpec((1, ATTN_BLOCK, width), lambda bi, h, qi: (bi, qi, h)),
        out_shape=jax.ShapeDtypeStruct((b, s, n_heads * HEAD_DIM), BF16),
        scratch_shapes=[pltpu.VMEM((ATTN_HEADS, ATTN_BLOCK, HEAD_DIM), F32),
                        pltpu.VMEM((ATTN_HEADS, ATTN_BLOCK, 1), F32)],
        compiler_params=_params("parallel", "parallel", "arbitrary"),
        name="stick_breaking",
    )(z, z, z, gain.reshape(1, 1, n_heads * HEAD_DIM).astype(F32))


def _store_residual(h, first_col_tile, gain_ref, h_ref, hg_ref, ssq_ref):
    h_ref[...] = h
    hg_ref[...] = (h * gain_ref[...]).astype(hg_ref.dtype)

    @pl.when(first_col_tile)
    def _():
        ssq_ref[...] = jnp.zeros_like(ssq_ref)

    ssq_ref[...] += jnp.sum(h * h, axis=-1, keepdims=True)


def _snake(i, j, n_j):
    return jnp.where((i & 1) == 0, j, n_j - 1 - j)


def _row_inv_rms(ssq_ref, width):
    return lax.rsqrt(ssq_ref[...] * (1.0 / width) + EPS)


def _residual_out(m, n):
    return (jax.ShapeDtypeStruct((m, n), F32), jax.ShapeDtypeStruct((m, n), BF16),
            jax.ShapeDtypeStruct((m, 1), F32))


def _out_proj_kernel(ya_ref, yb_ref, w_ref, x_ref, gain_ref, h_ref, hg_ref, ssq_ref):
    ka = ya_ref.shape[1]
    acc = jnp.dot(ya_ref[...], w_ref[:ka, :], preferred_element_type=F32)
    acc = acc + jnp.dot(yb_ref[...], w_ref[ka:, :], preferred_element_type=F32)
    _store_residual(x_ref[...] + acc, pl.program_id(1) == 0, gain_ref, h_ref, hg_ref, ssq_ref)


def _out_proj(ya, yb, w, x, next_gain):
    m, ka = ya.shape
    kb = yb.shape[1]
    n = w.shape[1]
    n_j = n // OUT_COL_TILE
    tile = pl.BlockSpec((ROW_TILE, OUT_COL_TILE), lambda i, j: (i, _snake(i, j, n_j)))
    return pl.pallas_call(
        _out_proj_kernel,
        grid=(m // ROW_TILE, n_j),
        in_specs=[pl.BlockSpec((ROW_TILE, ka), lambda i, j: (i, 0)),
                  pl.BlockSpec((ROW_TILE, kb), lambda i, j: (i, 0)),
                  pl.BlockSpec((ka + kb, OUT_COL_TILE), lambda i, j: (0, _snake(i, j, n_j))),
                  tile,
                  pl.BlockSpec((1, OUT_COL_TILE), lambda i, j: (0, _snake(i, j, n_j)))],
        out_specs=(tile, tile, pl.BlockSpec((ROW_TILE, 1), lambda i, j: (i, 0))),
        out_shape=_residual_out(m, n),
        compiler_params=_params("arbitrary", "arbitrary"),
        name="out_proj",
    )(ya, yb, w, x, next_gain.reshape(1, n).astype(F32))


def _mlp_up_kernel(a_ref, w_ref, ssq_ref, c0_ref, c1_ref, o_ref, c0_out_ref, c1_out_ref):
    _cast_rider(c0_ref, c0_out_ref)
    _cast_rider(c1_ref, c1_out_ref)
    acc = jnp.dot(a_ref[...], w_ref[...], preferred_element_type=F32)
    pre = acc * _row_inv_rms(ssq_ref, a_ref.shape[1])
    o_ref[...] = jnp.square(jnp.maximum(pre, 0.0)).astype(o_ref.dtype)


def _mlp_up(a, ssq, w, cast0, cast1):
    m, k = a.shape
    n = w.shape[1]
    n_i, n_j = m // ROW_TILE, n // COL_TILE
    split = n_j // 2
    riders = [_rider_spec(cast0, n_i, 0, split), _rider_spec(cast1, n_i, split, n_j)]
    return pl.pallas_call(
        _mlp_up_kernel,
        grid=(n_i, n_j),
        in_specs=[pl.BlockSpec((ROW_TILE, k), lambda i, j: (i, 0)),
                  pl.BlockSpec((k, COL_TILE), lambda i, j: (0, j)),
                  pl.BlockSpec((ROW_TILE, 1), lambda i, j: (i, 0))] + riders,
        out_specs=[pl.BlockSpec((ROW_TILE, COL_TILE), lambda i, j: (i, j))] + riders,
        out_shape=(jax.ShapeDtypeStruct((m, n), BF16),
                   jax.ShapeDtypeStruct(cast0.shape, BF16),
                   jax.ShapeDtypeStruct(cast1.shape, BF16)),
        compiler_params=_params("parallel", "arbitrary"),
        name="mlp_up",
    )(a, w, ssq, cast0, cast1)


def _mlp_down_kernel(a_hbm, w_hbm, r_ref, gain_ref, h_ref, hg_ref, ssq_ref,
                     a_ring, w_ring, sems, acc_ref, *, n_i, n_j, n_k):
    kk = pl.program_id(2)
    last = n_k - 1
    step = (pl.program_id(0) * n_j + pl.program_id(1)) * n_k + kk
    n_steps = n_i * n_j * n_k

    def tile_copies(s):
        slot = lax.rem(s, DOWN_RING)
        kk_s = lax.rem(s, n_k)
        ij_s = lax.div(s, n_k)
        rows = pl.ds(lax.div(ij_s, n_j) * ROW_TILE, ROW_TILE)
        cols = pl.ds(lax.rem(ij_s, n_j) * COL_TILE, COL_TILE)
        depth = pl.ds(kk_s * DOWN_K_TILE, DOWN_K_TILE)
        return (pltpu.make_async_copy(a_hbm.at[rows, depth], a_ring.at[slot], sems.at[0, slot]),
                pltpu.make_async_copy(w_hbm.at[depth, cols], w_ring.at[slot], sems.at[1, slot]))

    def start(s):
        for copy in tile_copies(s):
            copy.start()

    @pl.when(step == 0)
    def _():
        for s in range(RING_AHEAD):
            start(s)

    @pl.when(step + RING_AHEAD < n_steps)
    def _():
        start(step + RING_AHEAD)

    for copy in tile_copies(step):
        copy.wait()
    slot = lax.rem(step, DOWN_RING)

    def partial_product():
        return jnp.dot(a_ring[slot], w_ring[slot], preferred_element_type=F32)

    @pl.when(kk == 0)
    def _():
        acc_ref[...] = partial_product()

    @pl.when(jnp.logical_and(kk > 0, kk < last))
    def _():
        acc_ref[...] += partial_product()

    @pl.when(kk == last)
    def _():
        _store_residual(r_ref[...] + (acc_ref[...] + partial_product()), pl.program_id(1) == 0,
                        gain_ref, h_ref, hg_ref, ssq_ref)


def _mlp_down(a, w, r, next_gain):
    m, k = a.shape
    n = w.shape[1]
    n_i, n_j, n_k = m // ROW_TILE, n // COL_TILE, k // DOWN_K_TILE
    assert n_k >= 2, "first and last contraction steps must be distinct"
    assert n_i * n_j * n_k >= RING_AHEAD
    tile = pl.BlockSpec((ROW_TILE, COL_TILE), lambda i, j, kk: (i, j))
    return pl.pallas_call(
        functools.partial(_mlp_down_kernel, n_i=n_i, n_j=n_j, n_k=n_k),
        grid=(n_i, n_j, n_k),
        in_specs=[pl.BlockSpec(memory_space=pl.ANY),
                  pl.BlockSpec(memory_space=pl.ANY),
                  tile,
                  pl.BlockSpec((1, COL_TILE), lambda i, j, kk: (0, j))],
        out_specs=(tile, tile, pl.BlockSpec((ROW_TILE, 1), lambda i, j, kk: (i, 0))),
        out_shape=_residual_out(m, n),
        scratch_shapes=[pltpu.VMEM((DOWN_RING, ROW_TILE, DOWN_K_TILE), BF16),
                        pltpu.VMEM((DOWN_RING, DOWN_K_TILE, COL_TILE), BF16),
                        pltpu.SemaphoreType.DMA((2, DOWN_RING)),
                        pltpu.VMEM((ROW_TILE, COL_TILE), F32)],
        compiler_params=_params("arbitrary", "arbitrary", "arbitrary"),
        name="mlp_down",
    )(a, w, r, next_gain.reshape(1, n).astype(F32))


def _ple_kernel(a_ref, wg_ref, ssq_ref, p_ref, wp_ref, h_ref, gf_ref, out_hbm,
                rows_ref, sems, ssq_out_ref):
    i, j = pl.program_id(0), pl.program_id(1)
    last_i, last_j = pl.num_programs(0) - 1, pl.num_programs(1) - 1
    bm, n = rows_ref.shape[1], rows_ref.shape[2]
    slot = i & 1
    logits = jnp.dot(a_ref[...], wg_ref[...], preferred_element_type=F32)
    gate = 0.5 * jnp.tanh(logits * (0.5 * _row_inv_rms(ssq_ref, a_ref.shape[1]))) + 0.5
    e = jnp.dot(p_ref[...].astype(BF16), wp_ref[...], preferred_element_type=F32)
    h = h_ref[...] + gate * e
    bn = h.shape[1]
    col = _snake(i, j, pl.num_programs(1))
    rows_ref[slot, :, pl.ds(pl.multiple_of(col * bn, bn), bn)] = h

    @pl.when(j == 0)
    def _():
        ssq_out_ref[...] = jnp.zeros_like(ssq_out_ref)

    ssq_out_ref[...] += jnp.sum(h * h, axis=-1, keepdims=True)

    def write_back(row_block):
        s = row_block & 1
        return pltpu.make_async_copy(rows_ref.at[s], out_hbm.at[pl.ds(row_block * bm, bm), :],
                                     sems.at[s])

    @pl.when(j == last_j)
    def _():
        inv = _row_inv_rms(ssq_out_ref, n)
        for jj in range(n // bn):
            cols = slice(jj * bn, (jj + 1) * bn)
            rows_ref[slot, :, cols] = rows_ref[slot, :, cols] * inv * gf_ref[:, cols]

        @pl.when(i > 0)
        def _():
            write_back(i - 1).wait()

        write_back(i).start()

        @pl.when(i == last_i)
        def _():
            write_back(i).wait()


def _ple_final(a, ssq, w_gate, p, w_proj, h, final_gain):
    m, k = a.shape
    n = w_gate.shape[1]
    kp = p.shape[1]
    n_j = n // COL_TILE
    return pl.pallas_call(
        _ple_kernel,
        grid=(m // PLE_ROW_TILE, n_j),
        in_specs=[pl.BlockSpec((PLE_ROW_TILE, k), lambda i, j: (i, 0)),
                  pl.BlockSpec((k, COL_TILE), lambda i, j: (0, _snake(i, j, n_j))),
                  pl.BlockSpec((PLE_ROW_TILE, 1), lambda i, j: (i, 0)),
                  pl.BlockSpec((PLE_ROW_TILE, kp), lambda i, j: (i, 0)),
                  pl.BlockSpec((kp, COL_TILE), lambda i, j: (0, _snake(i, j, n_j))),
                  pl.BlockSpec((PLE_ROW_TILE, COL_TILE), lambda i, j: (i, _snake(i, j, n_j))),
                  pl.BlockSpec((1, n), lambda i, j: (0, 0))],
        out_specs=pl.BlockSpec(memory_space=pl.ANY),
        out_shape=jax.ShapeDtypeStruct((m, n), F32),
        scratch_shapes=[pltpu.VMEM((2, PLE_ROW_TILE, n), F32),
                        pltpu.SemaphoreType.DMA((2,)),
                        pltpu.VMEM((PLE_ROW_TILE, 1), F32)],
        compiler_params=_params("arbitrary", "arbitrary"),
        name="ple_gate",
    )(a, w_gate, ssq, p, w_proj, h, final_gain.reshape(1, n).astype(F32))


def kernel(x, p, norm_mix, w_in, w_s, b_s, norm_a_out, norm_b_out, w_out, norm_ffn, w_up,
           w_down, norm_ple, w_ple_gate, w_ple_proj, norm_final):
    batch, seq, d_model = x.shape
    assert w_in.shape[0] == 1, "single layer: the gated-embedding kernel applies the final rmsnorm"
    a_width = norm_a_out.shape[1]
    b_width = norm_b_out.shape[1]
    sgu_heads = a_width // HEAD_DIM
    sb_heads = b_width // HEAD_DIM
    m = batch * seq
    scale = LOG2E / math.sqrt(HEAD_DIM)
    o1, o2 = a_width, 2 * a_width
    o3, o4 = o2 + b_width, o2 + 2 * b_width

    h = x.reshape(m, d_model)
    a = _rmsnorm(h, norm_mix[0], BF16)
    z, w_up_bf, w_out_bf = _in_proj(a, w_in[0].astype(BF16), o2, o3, scale, w_up[0], w_out[0])
    y_a = _sgu(z, w_s[0], b_s[0], norm_a_out[0], sgu_heads, 0, o1)
    y_b = _attention(z.reshape(batch, seq, -1), norm_b_out[0], sb_heads, o2, o3, o4)
    h, hg, ssq = _out_proj(y_a, y_b.reshape(m, b_width), w_out_bf, h, norm_ffn[0])
    hid, w_down_bf, w_gate_bf = _mlp_up(hg, ssq, w_up_bf, w_down[0], w_ple_gate[0])
    h, hg, ssq = _mlp_down(hid, w_down_bf, h, norm_ple[0])
    out = _ple_final(hg, ssq, w_gate_bf, p[0].reshape(m, PLE_DIM),
                     w_ple_proj[0].astype(BF16), h, norm_final)
    return out.reshape(batch, seq, d_model).astype(x.dtype)
```

```python
import functools
import math

import jax
import jax.numpy as jnp
from jax import lax
from jax.experimental import pallas as pl
from jax.experimental.pallas import tpu as pltpu

F32 = jnp.float32
BF16 = jnp.bfloat16

EPS = 1e-6
HEAD_DIM = 128
CHUNK = 128
PLE_DIM = 256

ROW_TILE = 1024
COL_TILE = 1024
OUT_COL_TILE = 512
PLE_ROW_TILE = 512
DOWN_K_TILE = 2048
RING_AHEAD = 2
DOWN_RING = RING_AHEAD + 1
NORM_ROWS = 256
SGU_ROWS = 2048
ATTN_BLOCK = 256
ATTN_HEADS = 8
ATTN_SKIP_LOG2 = -160.0
MASKED_LOGIT = -1e30
LOG2E = 1.4426950408889634
VMEM_LIMIT = 56 * 1024 * 1024


def _params(*sem):
    return pltpu.CompilerParams(dimension_semantics=sem, vmem_limit_bytes=VMEM_LIMIT)


def _rmsnorm_kernel(x_ref, g_ref, o_ref):
    x = x_ref[...].astype(F32)
    inv = lax.rsqrt(jnp.mean(x * x, axis=-1, keepdims=True) + EPS)
    o_ref[...] = (x * inv * g_ref[...]).astype(o_ref.dtype)


def _rmsnorm(x, g, out_dtype):
    m, d = x.shape
    return pl.pallas_call(
        _rmsnorm_kernel,
        grid=(m // NORM_ROWS,),
        in_specs=[pl.BlockSpec((NORM_ROWS, d), lambda i: (i, 0)),
                  pl.BlockSpec((1, d), lambda i: (0, 0))],
        out_specs=pl.BlockSpec((NORM_ROWS, d), lambda i: (i, 0)),
        out_shape=jax.ShapeDtypeStruct((m, d), out_dtype),
        compiler_params=_params("parallel"),
        name="rmsnorm",
    )(x, g.reshape(1, d).astype(F32))


def _in_proj_kernel(a_ref, w_ref, c0_ref, c1_ref, o_ref, c0_out_ref, c1_out_ref, *,
                    q_lo, q_hi, scale, cast_split):
    j = pl.program_id(1)
    _cast_rider(c0_ref, c0_out_ref, j < cast_split)
    _cast_rider(c1_ref, c1_out_ref, j >= cast_split)
    acc = jnp.dot(a_ref[...], w_ref[...], preferred_element_type=F32)
    s = jnp.where(jnp.logical_and(j >= q_lo, j < q_hi), scale, 1.0).astype(F32)
    o_ref[...] = (acc * s).astype(o_ref.dtype)


def _cast_rider(src_ref, dst_ref, active):
    @pl.when(active)
    def _():
        dst_ref[...] = src_ref[...].astype(dst_ref.dtype)


def _rider_spec(w, n_i, j_lo, j_hi):
    rows, cols = w.shape[0] // n_i, w.shape[1] // (j_hi - j_lo)
    assert rows * n_i == w.shape[0] and cols * (j_hi - j_lo) == w.shape[1]
    return pl.BlockSpec((rows, cols), lambda i, j: (i, jnp.clip(j - j_lo, 0, j_hi - j_lo - 1)))


def _in_proj(a, w, q_col_lo, q_col_hi, scale, cast0, cast1):
    m, k = a.shape
    n = w.shape[1]
    n_i, n_j = m // ROW_TILE, n // COL_TILE
    split = n_j - max(1, n_j // 5)
    kern = functools.partial(_in_proj_kernel, q_lo=q_col_lo // COL_TILE,
                             q_hi=q_col_hi // COL_TILE, scale=scale, cast_split=split)
    riders = [_rider_spec(cast0, n_i, 0, split), _rider_spec(cast1, n_i, split, n_j)]
    return pl.pallas_call(
        kern,
        grid=(n_i, n_j),
        in_specs=[pl.BlockSpec((ROW_TILE, k), lambda i, j: (i, 0)),
                  pl.BlockSpec((k, COL_TILE), lambda i, j: (0, j))] + riders,
        out_specs=[pl.BlockSpec((ROW_TILE, COL_TILE), lambda i, j: (i, j))] + riders,
        out_shape=(jax.ShapeDtypeStruct((m, n), BF16),
                   jax.ShapeDtypeStruct(cast0.shape, BF16),
                   jax.ShapeDtypeStruct(cast1.shape, BF16)),
        compiler_params=_params("parallel", "arbitrary"),
        name="in_proj",
    )(a, w, cast0, cast1)


def _sgu_kernel(u_ref, v_ref, w_ref, b_ref, g_ref, o_ref):
    u = jax.nn.gelu(u_ref[...].astype(F32))
    v = jax.nn.gelu(v_ref[...].astype(F32))
    mu = jnp.mean(v, axis=-1, keepdims=True)
    vc = v - mu
    var = jnp.mean(vc * vc, axis=-1, keepdims=True)
    vn = (vc * lax.rsqrt(var + EPS)).astype(BF16)
    row = lax.broadcasted_iota(jnp.int32, (CHUNK, CHUNK), 0)
    col = lax.broadcasted_iota(jnp.int32, (CHUNK, CHUNK), 1)
    w = jnp.where(row >= col, w_ref[0], 0.0).astype(BF16)
    bias = b_ref[0]
    g = g_ref[0]
    for c in range(u.shape[0] // CHUNK):
        sl = slice(c * CHUNK, (c + 1) * CHUNK)
        mixed = jnp.dot(w, vn[sl], preferred_element_type=F32) + bias
        y = u[sl] * mixed
        inv = lax.rsqrt(jnp.mean(y * y, axis=-1, keepdims=True) + EPS)
        o_ref[sl, :] = (y * inv * g).astype(o_ref.dtype)


def _sgu(z, w_s, b_s, gain, n_heads, u_col0, v_col0):
    m = z.shape[0]
    ub, vb = u_col0 // HEAD_DIM, v_col0 // HEAD_DIM
    bias = jnp.broadcast_to(b_s.astype(F32)[:, :, None], (n_heads, CHUNK, HEAD_DIM))
    return pl.pallas_call(
        _sgu_kernel,
        grid=(m // SGU_ROWS, n_heads),
        in_specs=[pl.BlockSpec((SGU_ROWS, HEAD_DIM), lambda i, h: (i, ub + h)),
                  pl.BlockSpec((SGU_ROWS, HEAD_DIM), lambda i, h: (i, vb + h)),
                  pl.BlockSpec((1, CHUNK, CHUNK), lambda i, h: (h, 0, 0)),
                  pl.BlockSpec((1, CHUNK, HEAD_DIM), lambda i, h: (h, 0, 0)),
                  pl.BlockSpec((1, 1, HEAD_DIM), lambda i, h: (h, 0, 0))],
        out_specs=pl.BlockSpec((SGU_ROWS, HEAD_DIM), lambda i, h: (i, h)),
        out_shape=jax.ShapeDtypeStruct((m, n_heads * HEAD_DIM), BF16),
        compiler_params=_params("parallel", "parallel"),
        name="sgu",
    )(z, z, w_s.astype(F32), bias, gain.reshape(n_heads, 1, HEAD_DIM).astype(F32))


def _attn_kernel(q_ref, k_ref, v_ref, g_ref, o_ref, acc_ref, carry_ref):
    blk = ATTN_BLOCK
    heads = range(q_ref.shape[2] // HEAD_DIM)
    qi = pl.program_id(2)
    row = lax.broadcasted_iota(jnp.int32, (blk, blk), 0)
    col = lax.broadcasted_iota(jnp.int32, (blk, blk), 1)
    after = jnp.where(row > col, 1.0, 0.0).astype(BF16)
    causal = col < row

    def lanes(h):
        return slice(h * HEAD_DIM, (h + 1) * HEAD_DIM)

    def scores(h, kb):
        k = k_ref[0, pl.ds(kb * blk, blk), lanes(h)]
        return lax.dot_general(q_ref[0, :, lanes(h)], k, (((1,), (1,)), ((), ())),
                               preferred_element_type=F32)

    def log_terms(z, masked):
        if masked:
            z = jnp.where(causal, z, MASKED_LOGIT)
        t = jnp.log(1.0 + jnp.exp2(-jnp.abs(z))) * LOG2E
        log_beta = jnp.minimum(z, 0.0) - t
        return log_beta, log_beta - z

    def attend(tiles, first):
        z = [[scores(h, kb) for kb, _ in tiles] for h in heads]
        logs = [[log_terms(z[h][t], tiles[t][1]) for t in range(len(tiles))] for h in heads]
        sums = [[jnp.dot(lg[1].astype(BF16), after, preferred_element_type=F32) for lg in logs[h]]
                for h in heads]
        weights = []
        for h in heads:
            carry = None if first else carry_ref[h]
            row_w = []
            for t in range(len(tiles)):
                log_beta, log_keep = logs[h][t]
                log_after = sums[h][t] if carry is None else sums[h][t] + carry
                row_w.append(jnp.exp2(log_beta + log_after).astype(BF16))
                block_sum = jnp.sum(log_keep, axis=-1, keepdims=True)
                carry = block_sum if carry is None else carry + block_sum
            carry_ref[h] = carry
            weights.append(row_w)
        for h in heads:
            out = None
            for t, (kb, _) in enumerate(tiles):
                v = v_ref[0, pl.ds(kb * blk, blk), lanes(h)]
                part = jnp.dot(weights[h][t], v, preferred_element_type=F32)
                out = part if out is None else out + part
            acc_ref[h] = out if first else acc_ref[h] + out

    def live():
        top = carry_ref[0]
        for h in heads[1:]:
            top = jnp.maximum(top, carry_ref[h])
        return (jnp.max(top) > ATTN_SKIP_LOG2).astype(jnp.int32)

    @pl.when(qi == 0)
    def _():
        attend([(qi, True)], True)

    @pl.when(qi > 0)
    def _():
        attend([(qi, True), (qi - 1, False)], True)

        def cond(state):
            return jnp.logical_and(state[0] >= 0, state[1] > 0)

        def body(state):
            attend([(state[0], False)], False)
            return state[0] - 1, live()

        lax.while_loop(cond, body, (qi - 2, live()))

    for h in heads:
        acc = acc_ref[h]
        inv = lax.rsqrt(jnp.mean(acc * acc, axis=-1, keepdims=True) + EPS)
        o_ref[0, :, lanes(h)] = (acc * inv * g_ref[0, :, lanes(h)]).astype(o_ref.dtype)


def _attention(z, gain, n_heads, q_col0, k_col0, v_col0):
    b, s, _ = z.shape
    width = ATTN_HEADS * HEAD_DIM
    qb, kb, vb = q_col0 // width, k_col0 // width, v_col0 // width
    return pl.pallas_call(
        _attn_kernel,
        grid=(b, n_heads // ATTN_HEADS, s // ATTN_BLOCK),
        in_specs=[pl.BlockSpec((1, ATTN_BLOCK, width), lambda bi, h, qi: (bi, qi, qb + h)),
                  pl.BlockSpec((1, s, width), lambda bi, h, qi: (bi, 0, kb + h)),
                  pl.BlockSpec((1, s, width), lambda bi, h, qi: (bi, 0, vb + h)),
                  pl.BlockSpec((1, 1, width), lambda bi, h, qi: (0, 0, h))],
        out_specs=pl.BlockSpec((1, ATTN_BLOCK, width), lambda bi, h, qi: (bi, qi, h)),
        out_shape=jax.ShapeDtypeStruct((b, s, n_heads * HEAD_DIM), BF16),
        scratch_shapes=[pltpu.VMEM((ATTN_HEADS, ATTN_BLOCK, HEAD_DIM), F32),
                        pltpu.VMEM((ATTN_HEADS, ATTN_BLOCK, 1), F32)],
        compiler_params=_params("parallel", "parallel", "arbitrary"),
        name="stick_breaking",
    )(z, z, z, gain.reshape(1, 1, n_heads * HEAD_DIM).astype(F32))


def _store_residual(h, first_col_tile, gain_ref, h_ref, hg_ref, ssq_ref):
    h_ref[...] = h
    hg_ref[...] = (h * gain_ref[...]).astype(hg_ref.dtype)

    @pl.when(first_col_tile)
    def _():
        ssq_ref[...] = jnp.zeros_like(ssq_ref)

    ssq_ref[...] += jnp.sum(h * h, axis=-1, keepdims=True)


def _snake(i, j, n_j):
    return jnp.where((i & 1) == 0, j, n_j - 1 - j)


def _row_inv_rms(ssq_ref, width):
    return lax.rsqrt(ssq_ref[...] * (1.0 / width) + EPS)


def _residual_out(m, n):
    return (jax.ShapeDtypeStruct((m, n), F32), jax.ShapeDtypeStruct((m, n), BF16),
            jax.ShapeDtypeStruct((m, 1), F32))


def _out_proj_kernel(ya_ref, yb_ref, w_ref, x_ref, gain_ref, h_ref, hg_ref, ssq_ref):
    ka = ya_ref.shape[1]
    acc = jnp.dot(ya_ref[...], w_ref[:ka, :], preferred_element_type=F32)
    acc = acc + jnp.dot(yb_ref[...], w_ref[ka:, :], preferred_element_type=F32)
    _store_residual(x_ref[...] + acc, pl.program_id(1) == 0, gain_ref, h_ref, hg_ref, ssq_ref)


def _out_proj(ya, yb, w, x, next_gain):
    m, ka = ya.shape
    kb = yb.shape[1]
    n = w.shape[1]
    n_j = n // OUT_COL_TILE
    tile = pl.BlockSpec((ROW_TILE, OUT_COL_TILE), lambda i, j: (i, _snake(i, j, n_j)))
    return pl.pallas_call(
        _out_proj_kernel,
        grid=(m // ROW_TILE, n_j),
        in_specs=[pl.BlockSpec((ROW_TILE, ka), lambda i, j: (i, 0)),
                  pl.BlockSpec((ROW_TILE, kb), lambda i, j: (i, 0)),
                  pl.BlockSpec((ka + kb, OUT_COL_TILE), lambda i, j: (0, _snake(i, j, n_j))),
                  tile,
                  pl.BlockSpec((1, OUT_COL_TILE), lambda i, j: (0, _snake(i, j, n_j)))],
        out_specs=(tile, tile, pl.BlockSpec((ROW_TILE, 1), lambda i, j: (i, 0))),
        out_shape=_residual_out(m, n),
        compiler_params=_params("arbitrary", "arbitrary"),
        name="out_proj",
    )(ya, yb, w, x, next_gain.reshape(1, n).astype(F32))


def _mlp_up_kernel(a_ref, w_ref, ssq_ref, c0_ref, c1_ref, o_ref, c0_out_ref, c1_out_ref, *,
                   cast_split):
    j = pl.program_id(1)
    _cast_rider(c0_ref, c0_out_ref, j < cast_split)
    _cast_rider(c1_ref, c1_out_ref, j >= cast_split)
    acc = jnp.dot(a_ref[...], w_ref[...], preferred_element_type=F32)
    pre = acc * _row_inv_rms(ssq_ref, a_ref.shape[1])
    o_ref[...] = jnp.square(jnp.maximum(pre, 0.0)).astype(o_ref.dtype)


def _mlp_up(a, ssq, w, cast0, cast1):
    m, k = a.shape
    n = w.shape[1]
    n_i, n_j = m // ROW_TILE, n // COL_TILE
    split = n_j // 2
    riders = [_rider_spec(cast0, n_i, 0, split), _rider_spec(cast1, n_i, split, n_j)]
    return pl.pallas_call(
        functools.partial(_mlp_up_kernel, cast_split=split),
        grid=(n_i, n_j),
        in_specs=[pl.BlockSpec((ROW_TILE, k), lambda i, j: (i, 0)),
                  pl.BlockSpec((k, COL_TILE), lambda i, j: (0, j)),
                  pl.BlockSpec((ROW_TILE, 1), lambda i, j: (i, 0))] + riders,
        out_specs=[pl.BlockSpec((ROW_TILE, COL_TILE), lambda i, j: (i, j))] + riders,
        out_shape=(jax.ShapeDtypeStruct((m, n), BF16),
                   jax.ShapeDtypeStruct(cast0.shape, BF16),
                   jax.ShapeDtypeStruct(cast1.shape, BF16)),
        compiler_params=_params("parallel", "arbitrary"),
        name="mlp_up",
    )(a, w, ssq, cast0, cast1)


def _mlp_down_kernel(a_hbm, w_hbm, r_ref, gain_ref, h_ref, hg_ref, ssq_ref,
                     a_ring, w_ring, sems, acc_ref, *, n_i, n_j, n_k):
    kk = pl.program_id(2)
    last = n_k - 1
    step = (pl.program_id(0) * n_j + pl.program_id(1)) * n_k + kk
    n_steps = n_i * n_j * n_k

    def tile_copies(s):
        slot = lax.rem(s, DOWN_RING)
        kk_s = lax.rem(s, n_k)
        ij_s = lax.div(s, n_k)
        rows = pl.ds(lax.div(ij_s, n_j) * ROW_TILE, ROW_TILE)
        cols = pl.ds(lax.rem(ij_s, n_j) * COL_TILE, COL_TILE)
        depth = pl.ds(kk_s * DOWN_K_TILE, DOWN_K_TILE)
        return (pltpu.make_async_copy(a_hbm.at[rows, depth], a_ring.at[slot], sems.at[0, slot]),
                pltpu.make_async_copy(w_hbm.at[depth, cols], w_ring.at[slot], sems.at[1, slot]))

    def start(s):
        for copy in tile_copies(s):
            copy.start()

    @pl.when(step == 0)
    def _():
        for s in range(RING_AHEAD):
            start(s)

    @pl.when(step + RING_AHEAD < n_steps)
    def _():
        start(step + RING_AHEAD)

    for copy in tile_copies(step):
        copy.wait()
    slot = lax.rem(step, DOWN_RING)

    def partial_product():
        return jnp.dot(a_ring[slot], w_ring[slot], preferred_element_type=F32)

    @pl.when(kk == 0)
    def _():
        acc_ref[...] = partial_product()

    @pl.when(jnp.logical_and(kk > 0, kk < last))
    def _():
        acc_ref[...] += partial_product()

    @pl.when(kk == last)
    def _():
        _store_residual(r_ref[...] + (acc_ref[...] + partial_product()), pl.program_id(1) == 0,
                        gain_ref, h_ref, hg_ref, ssq_ref)


def _mlp_down(a, w, r, next_gain):
    m, k = a.shape
    n = w.shape[1]
    n_i, n_j, n_k = m // ROW_TILE, n // COL_TILE, k // DOWN_K_TILE
    assert n_k >= 2, "first and last contraction steps must be distinct"
    assert n_i * n_j * n_k >= RING_AHEAD
    tile = pl.BlockSpec((ROW_TILE, COL_TILE), lambda i, j, kk: (i, j))
    return pl.pallas_call(
        functools.partial(_mlp_down_kernel, n_i=n_i, n_j=n_j, n_k=n_k),
        grid=(n_i, n_j, n_k),
        in_specs=[pl.BlockSpec(memory_space=pl.ANY),
                  pl.BlockSpec(memory_space=pl.ANY),
                  tile,
                  pl.BlockSpec((1, COL_TILE), lambda i, j, kk: (0, j))],
        out_specs=(tile, tile, pl.BlockSpec((ROW_TILE, 1), lambda i, j, kk: (i, 0))),
        out_shape=_residual_out(m, n),
        scratch_shapes=[pltpu.VMEM((DOWN_RING, ROW_TILE, DOWN_K_TILE), BF16),
                        pltpu.VMEM((DOWN_RING, DOWN_K_TILE, COL_TILE), BF16),
                        pltpu.SemaphoreType.DMA((2, DOWN_RING)),
                        pltpu.VMEM((ROW_TILE, COL_TILE), F32)],
        compiler_params=_params("arbitrary", "arbitrary", "arbitrary"),
        name="mlp_down",
    )(a, w, r, next_gain.reshape(1, n).astype(F32))


def _ple_kernel(a_ref, wg_ref, ssq_ref, p_ref, wp_ref, h_ref, gf_ref, o_ref, ssq_out_ref):
    j = pl.program_id(1)
    logits = jnp.dot(a_ref[...], wg_ref[...], preferred_element_type=F32)
    gate = 0.5 * jnp.tanh(logits * (0.5 * _row_inv_rms(ssq_ref, a_ref.shape[1]))) + 0.5
    e = jnp.dot(p_ref[...].astype(BF16), wp_ref[...], preferred_element_type=F32)
    h = h_ref[...] + gate * e
    bn = h.shape[1]
    col = _snake(pl.program_id(0), j, pl.num_programs(1))
    o_ref[:, pl.ds(pl.multiple_of(col * bn, bn), bn)] = h

    @pl.when(j == 0)
    def _():
        ssq_out_ref[...] = jnp.zeros_like(ssq_out_ref)

    ssq_out_ref[...] += jnp.sum(h * h, axis=-1, keepdims=True)

    @pl.when(j == pl.num_programs(1) - 1)
    def _():
        inv = _row_inv_rms(ssq_out_ref, o_ref.shape[1])
        for jj in range(o_ref.shape[1] // bn):
            cols = slice(jj * bn, (jj + 1) * bn)
            o_ref[:, cols] = o_ref[:, cols] * inv * gf_ref[:, cols]


def _ple_final(a, ssq, w_gate, p, w_proj, h, final_gain):
    m, k = a.shape
    n = w_gate.shape[1]
    kp = p.shape[1]
    n_j = n // COL_TILE
    return pl.pallas_call(
        _ple_kernel,
        grid=(m // PLE_ROW_TILE, n_j),
        in_specs=[pl.BlockSpec((PLE_ROW_TILE, k), lambda i, j: (i, 0)),
                  pl.BlockSpec((k, COL_TILE), lambda i, j: (0, _snake(i, j, n_j))),
                  pl.BlockSpec((PLE_ROW_TILE, 1), lambda i, j: (i, 0)),
                  pl.BlockSpec((PLE_ROW_TILE, kp), lambda i, j: (i, 0)),
                  pl.BlockSpec((kp, COL_TILE), lambda i, j: (0, _snake(i, j, n_j))),
                  pl.BlockSpec((PLE_ROW_TILE, COL_TILE), lambda i, j: (i, _snake(i, j, n_j))),
                  pl.BlockSpec((1, n), lambda i, j: (0, 0))],
        out_specs=pl.BlockSpec((PLE_ROW_TILE, n), lambda i, j: (i, 0)),
        out_shape=jax.ShapeDtypeStruct((m, n), F32),
        scratch_shapes=[pltpu.VMEM((PLE_ROW_TILE, 1), F32)],
        compiler_params=_params("arbitrary", "arbitrary"),
        name="ple_gate",
    )(a, w_gate, ssq, p, w_proj, h, final_gain.reshape(1, n).astype(F32))


def kernel(x, p, norm_mix, w_in, w_s, b_s, norm_a_out, norm_b_out, w_out, norm_ffn, w_up,
           w_down, norm_ple, w_ple_gate, w_ple_proj, norm_final):
    batch, seq, d_model = x.shape
    assert w_in.shape[0] == 1, "single layer: the gated-embedding kernel applies the final rmsnorm"
    a_width = norm_a_out.shape[1]
    b_width = norm_b_out.shape[1]
    sgu_heads = a_width // HEAD_DIM
    sb_heads = b_width // HEAD_DIM
    m = batch * seq
    scale = LOG2E / math.sqrt(HEAD_DIM)
    o1, o2 = a_width, 2 * a_width
    o3, o4 = o2 + b_width, o2 + 2 * b_width

    h = x.reshape(m, d_model)
    a = _rmsnorm(h, norm_mix[0], BF16)
    z, w_up_bf, w_out_bf = _in_proj(a, w_in[0].astype(BF16), o2, o3, scale, w_up[0], w_out[0])
    y_a = _sgu(z, w_s[0], b_s[0], norm_a_out[0], sgu_heads, 0, o1)
    y_b = _attention(z.reshape(batch, seq, -1), norm_b_out[0], sb_heads, o2, o3, o4)
    h, hg, ssq = _out_proj(y_a, y_b.reshape(m, b_width), w_out_bf, h, norm_ffn[0])
    hid, w_down_bf, w_gate_bf = _mlp_up(hg, ssq, w_up_bf, w_down[0], w_ple_gate[0])
    h, hg, ssq = _mlp_down(hid, w_down_bf, h, norm_ple[0])
    out = _ple_final(hg, ssq, w_gate_bf, p[0].reshape(m, PLE_DIM),
                     w_ple_proj[0].astype(BF16), h, norm_final)
    return out.reshape(batch, seq, d_model).astype(x.dtype)
```

```python
import functools
import math

import jax
import jax.numpy as jnp
from jax import lax
from jax.experimental import pallas as pl
from jax.experimental.pallas import tpu as pltpu

F32 = jnp.float32
BF16 = jnp.bfloat16

EPS = 1e-6
HEAD_DIM = 128
CHUNK = 128
PLE_DIM = 256

ROW_TILE = 1024
COL_TILE = 1024
OUT_COL_TILE = 512
PLE_ROW_TILE = 1024
PLE_COL_TILE = 512
PLE_VMEM_LIMIT = 62 * 1024 * 1024
DOWN_K_TILE = 2048
RING_AHEAD = 2
DOWN_RING = RING_AHEAD + 1
NORM_ROWS = 256
SGU_ROWS = 2048
ATTN_BLOCK = 256
ATTN_HEADS = 8
ATTN_SKIP_LOG2 = -160.0
MASKED_LOGIT = -1e30
LOG2E = 1.4426950408889634
VMEM_LIMIT = 56 * 1024 * 1024


def _params(*sem):
    return pltpu.CompilerParams(dimension_semantics=sem, vmem_limit_bytes=VMEM_LIMIT)


def _rmsnorm_kernel(x_ref, g_ref, o_ref):
    x = x_ref[...].astype(F32)
    inv = lax.rsqrt(jnp.mean(x * x, axis=-1, keepdims=True) + EPS)
    o_ref[...] = (x * inv * g_ref[...]).astype(o_ref.dtype)


def _rmsnorm(x, g, out_dtype):
    m, d = x.shape
    return pl.pallas_call(
        _rmsnorm_kernel,
        grid=(m // NORM_ROWS,),
        in_specs=[pl.BlockSpec((NORM_ROWS, d), lambda i: (i, 0)),
                  pl.BlockSpec((1, d), lambda i: (0, 0))],
        out_specs=pl.BlockSpec((NORM_ROWS, d), lambda i: (i, 0)),
        out_shape=jax.ShapeDtypeStruct((m, d), out_dtype),
        compiler_params=_params("parallel"),
        name="rmsnorm",
    )(x, g.reshape(1, d).astype(F32))


def _in_proj_kernel(a_ref, w_ref, c0_ref, c1_ref, o_ref, c0_out_ref, c1_out_ref, *,
                    q_lo, q_hi, scale, cast_split):
    j = pl.program_id(1)
    _cast_rider(c0_ref, c0_out_ref, j < cast_split)
    _cast_rider(c1_ref, c1_out_ref, j >= cast_split)
    acc = jnp.dot(a_ref[...], w_ref[...], preferred_element_type=F32)
    s = jnp.where(jnp.logical_and(j >= q_lo, j < q_hi), scale, 1.0).astype(F32)
    o_ref[...] = (acc * s).astype(o_ref.dtype)


def _cast_rider(src_ref, dst_ref, active):
    @pl.when(active)
    def _():
        dst_ref[...] = src_ref[...].astype(dst_ref.dtype)


def _rider_spec(w, n_i, j_lo, j_hi):
    rows, cols = w.shape[0] // n_i, w.shape[1] // (j_hi - j_lo)
    assert rows * n_i == w.shape[0] and cols * (j_hi - j_lo) == w.shape[1]
    return pl.BlockSpec((rows, cols), lambda i, j: (i, jnp.clip(j - j_lo, 0, j_hi - j_lo - 1)))


def _in_proj(a, w, q_col_lo, q_col_hi, scale, cast0, cast1):
    m, k = a.shape
    n = w.shape[1]
    n_i, n_j = m // ROW_TILE, n // COL_TILE
    split = n_j - max(1, n_j // 5)
    kern = functools.partial(_in_proj_kernel, q_lo=q_col_lo // COL_TILE,
                             q_hi=q_col_hi // COL_TILE, scale=scale, cast_split=split)
    riders = [_rider_spec(cast0, n_i, 0, split), _rider_spec(cast1, n_i, split, n_j)]
    return pl.pallas_call(
        kern,
        grid=(n_i, n_j),
        in_specs=[pl.BlockSpec((ROW_TILE, k), lambda i, j: (i, 0)),
                  pl.BlockSpec((k, COL_TILE), lambda i, j: (0, j))] + riders,
        out_specs=[pl.BlockSpec((ROW_TILE, COL_TILE), lambda i, j: (i, j))] + riders,
        out_shape=(jax.ShapeDtypeStruct((m, n), BF16),
                   jax.ShapeDtypeStruct(cast0.shape, BF16),
                   jax.ShapeDtypeStruct(cast1.shape, BF16)),
        compiler_params=_params("parallel", "arbitrary"),
        name="in_proj",
    )(a, w, cast0, cast1)


def _sgu_kernel(u_ref, v_ref, w_ref, b_ref, g_ref, o_ref):
    u = jax.nn.gelu(u_ref[...].astype(F32))
    v = jax.nn.gelu(v_ref[...].astype(F32))
    mu = jnp.mean(v, axis=-1, keepdims=True)
    vc = v - mu
    var = jnp.mean(vc * vc, axis=-1, keepdims=True)
    vn = (vc * lax.rsqrt(var + EPS)).astype(BF16)
    row = lax.broadcasted_iota(jnp.int32, (CHUNK, CHUNK), 0)
    col = lax.broadcasted_iota(jnp.int32, (CHUNK, CHUNK), 1)
    w = jnp.where(row >= col, w_ref[0], 0.0).astype(BF16)
    bias = b_ref[0]
    g = g_ref[0]
    for c in range(u.shape[0] // CHUNK):
        sl = slice(c * CHUNK, (c + 1) * CHUNK)
        mixed = jnp.dot(w, vn[sl], preferred_element_type=F32) + bias
        y = u[sl] * mixed
        inv = lax.rsqrt(jnp.mean(y * y, axis=-1, keepdims=True) + EPS)
        o_ref[sl, :] = (y * inv * g).astype(o_ref.dtype)


def _sgu(z, w_s, b_s, gain, n_heads, u_col0, v_col0):
    m = z.shape[0]
    ub, vb = u_col0 // HEAD_DIM, v_col0 // HEAD_DIM
    bias = jnp.broadcast_to(b_s.astype(F32)[:, :, None], (n_heads, CHUNK, HEAD_DIM))
    return pl.pallas_call(
        _sgu_kernel,
        grid=(m // SGU_ROWS, n_heads),
        in_specs=[pl.BlockSpec((SGU_ROWS, HEAD_DIM), lambda i, h: (i, ub + h)),
                  pl.BlockSpec((SGU_ROWS, HEAD_DIM), lambda i, h: (i, vb + h)),
                  pl.BlockSpec((1, CHUNK, CHUNK), lambda i, h: (h, 0, 0)),
                  pl.BlockSpec((1, CHUNK, HEAD_DIM), lambda i, h: (h, 0, 0)),
                  pl.BlockSpec((1, 1, HEAD_DIM), lambda i, h: (h, 0, 0))],
        out_specs=pl.BlockSpec((SGU_ROWS, HEAD_DIM), lambda i, h: (i, h)),
        out_shape=jax.ShapeDtypeStruct((m, n_heads * HEAD_DIM), BF16),
        compiler_params=_params("parallel", "parallel"),
        name="sgu",
    )(z, z, w_s.astype(F32), bias, gain.reshape(n_heads, 1, HEAD_DIM).astype(F32))


def _attn_kernel(q_ref, k_ref, v_ref, g_ref, o_ref, acc_ref, carry_ref):
    blk = ATTN_BLOCK
    heads = range(q_ref.shape[2] // HEAD_DIM)
    qi = pl.program_id(2)
    row = lax.broadcasted_iota(jnp.int32, (blk, blk), 0)
    col = lax.broadcasted_iota(jnp.int32, (blk, blk), 1)
    after = jnp.where(row > col, 1.0, 0.0).astype(BF16)
    causal = col < row

    def lanes(h):
        return slice(h * HEAD_DIM, (h + 1) * HEAD_DIM)

    def scores(h, kb):
        k = k_ref[0, pl.ds(kb * blk, blk), lanes(h)]
        return lax.dot_general(q_ref[0, :, lanes(h)], k, (((1,), (1,)), ((), ())),
                               preferred_element_type=F32)

    def log_terms(z, masked):
        if masked:
            z = jnp.where(causal, z, MASKED_LOGIT)
        t = jnp.log(1.0 + jnp.exp2(-jnp.abs(z))) * LOG2E
        log_beta = jnp.minimum(z, 0.0) - t
        return log_beta, log_beta - z

    def attend(tiles, first):
        z = [[scores(h, kb) for kb, _ in tiles] for h in heads]
        logs = [[log_terms(z[h][t], tiles[t][1]) for t in range(len(tiles))] for h in heads]
        sums = [[jnp.dot(lg[1].astype(BF16), after, preferred_element_type=F32) for lg in logs[h]]
                for h in heads]
        weights = []
        for h in heads:
            carry = None if first else carry_ref[h]
            row_w = []
            for t in range(len(tiles)):
                log_beta, log_keep = logs[h][t]
                log_after = sums[h][t] if carry is None else sums[h][t] + carry
                row_w.append(jnp.exp2(log_beta + log_after).astype(BF16))
                block_sum = jnp.sum(log_keep, axis=-1, keepdims=True)
                carry = block_sum if carry is None else carry + block_sum
            carry_ref[h] = carry
            weights.append(row_w)
        for h in heads:
            out = None
            for t, (kb, _) in enumerate(tiles):
                v = v_ref[0, pl.ds(kb * blk, blk), lanes(h)]
                part = jnp.dot(weights[h][t], v, preferred_element_type=F32)
                out = part if out is None else out + part
            acc_ref[h] = out if first else acc_ref[h] + out

    def live():
        top = carry_ref[0]
        for h in heads[1:]:
            top = jnp.maximum(top, carry_ref[h])
        return (jnp.max(top) > ATTN_SKIP_LOG2).astype(jnp.int32)

    @pl.when(qi == 0)
    def _():
        attend([(qi, True)], True)

    @pl.when(qi > 0)
    def _():
        attend([(qi, True), (qi - 1, False)], True)

        def cond(state):
            return jnp.logical_and(state[0] >= 0, state[1] > 0)

        def body(state):
            attend([(state[0], False)], False)
            return state[0] - 1, live()

        lax.while_loop(cond, body, (qi - 2, live()))

    for h in heads:
        acc = acc_ref[h]
        inv = lax.rsqrt(jnp.mean(acc * acc, axis=-1, keepdims=True) + EPS)
        o_ref[0, :, lanes(h)] = (acc * inv * g_ref[0, :, lanes(h)]).astype(o_ref.dtype)


def _attention(z, gain, n_heads, q_col0, k_col0, v_col0):
    b, s, _ = z.shape
    width = ATTN_HEADS * HEAD_DIM
    qb, kb, vb = q_col0 // width, k_col0 // width, v_col0 // width
    return pl.pallas_call(
        _attn_kernel,
        grid=(b, n_heads // ATTN_HEADS, s // ATTN_BLOCK),
        in_specs=[pl.BlockSpec((1, ATTN_BLOCK, width), lambda bi, h, qi: (bi, qi, qb + h)),
                  pl.BlockSpec((1, s, width), lambda bi, h, qi: (bi, 0, kb + h)),
                  pl.BlockSpec((1, s, width), lambda bi, h, qi: (bi, 0, vb + h)),
                  pl.BlockSpec((1, 1, width), lambda bi, h, qi: (0, 0, h))],
        out_specs=pl.BlockSpec((1, ATTN_BLOCK, width), lambda bi, h, qi: (bi, qi, h)),
        out_shape=jax.ShapeDtypeStruct((b, s, n_heads * HEAD_DIM), BF16),
        scratch_shapes=[pltpu.VMEM((ATTN_HEADS, ATTN_BLOCK, HEAD_DIM), F32),
                        pltpu.VMEM((ATTN_HEADS, ATTN_BLOCK, 1), F32)],
        compiler_params=_params("parallel", "parallel", "arbitrary"),
        name="stick_breaking",
    )(z, z, z, gain.reshape(1, 1, n_heads * HEAD_DIM).astype(F32))


def _store_residual(h, first_col_tile, gain_ref, h_ref, hg_ref, ssq_ref):
    h_ref[...] = h
    hg_ref[...] = (h * gain_ref[...]).astype(hg_ref.dtype)

    @pl.when(first_col_tile)
    def _():
        ssq_ref[...] = jnp.zeros_like(ssq_ref)

    ssq_ref[...] += jnp.sum(h * h, axis=-1, keepdims=True)


def _snake(i, j, n_j):
    return jnp.where((i & 1) == 0, j, n_j - 1 - j)


def _row_inv_rms(ssq_ref, width):
    return lax.rsqrt(ssq_ref[...] * (1.0 / width) + EPS)


def _residual_out(m, n):
    return (jax.ShapeDtypeStruct((m, n), F32), jax.ShapeDtypeStruct((m, n), BF16),
            jax.ShapeDtypeStruct((m, 1), F32))


def _out_proj_kernel(ya_ref, yb_ref, w_ref, x_ref, gain_ref, h_ref, hg_ref, ssq_ref):
    ka = ya_ref.shape[1]
    acc = jnp.dot(ya_ref[...], w_ref[:ka, :], preferred_element_type=F32)
    acc = acc + jnp.dot(yb_ref[...], w_ref[ka:, :], preferred_element_type=F32)
    _store_residual(x_ref[...] + acc, pl.program_id(1) == 0, gain_ref, h_ref, hg_ref, ssq_ref)


def _out_proj(ya, yb, w, x, next_gain):
    m, ka = ya.shape
    kb = yb.shape[1]
    n = w.shape[1]
    n_j = n // OUT_COL_TILE
    tile = pl.BlockSpec((ROW_TILE, OUT_COL_TILE), lambda i, j: (i, _snake(i, j, n_j)))
    return pl.pallas_call(
        _out_proj_kernel,
        grid=(m // ROW_TILE, n_j),
        in_specs=[pl.BlockSpec((ROW_TILE, ka), lambda i, j: (i, 0)),
                  pl.BlockSpec((ROW_TILE, kb), lambda i, j: (i, 0)),
                  pl.BlockSpec((ka + kb, OUT_COL_TILE), lambda i, j: (0, _snake(i, j, n_j))),
                  tile,
                  pl.BlockSpec((1, OUT_COL_TILE), lambda i, j: (0, _snake(i, j, n_j)))],
        out_specs=(tile, tile, pl.BlockSpec((ROW_TILE, 1), lambda i, j: (i, 0))),
        out_shape=_residual_out(m, n),
        compiler_params=_params("arbitrary", "arbitrary"),
        name="out_proj",
    )(ya, yb, w, x, next_gain.reshape(1, n).astype(F32))


def _mlp_up_kernel(a_ref, w_ref, ssq_ref, c0_ref, c1_ref, o_ref, c0_out_ref, c1_out_ref, *,
                   cast_split):
    j = pl.program_id(1)
    _cast_rider(c0_ref, c0_out_ref, j < cast_split)
    _cast_rider(c1_ref, c1_out_ref, j >= cast_split)
    acc = jnp.dot(a_ref[...], w_ref[...], preferred_element_type=F32)
    pre = acc * _row_inv_rms(ssq_ref, a_ref.shape[1])
    o_ref[...] = jnp.square(jnp.maximum(pre, 0.0)).astype(o_ref.dtype)


def _mlp_up(a, ssq, w, cast0, cast1):
    m, k = a.shape
    n = w.shape[1]
    n_i, n_j = m // ROW_TILE, n // COL_TILE
    split = n_j // 2
    riders = [_rider_spec(cast0, n_i, 0, split), _rider_spec(cast1, n_i, split, n_j)]
    return pl.pallas_call(
        functools.partial(_mlp_up_kernel, cast_split=split),
        grid=(n_i, n_j),
        in_specs=[pl.BlockSpec((ROW_TILE, k), lambda i, j: (i, 0)),
                  pl.BlockSpec((k, COL_TILE), lambda i, j: (0, j)),
                  pl.BlockSpec((ROW_TILE, 1), lambda i, j: (i, 0))] + riders,
        out_specs=[pl.BlockSpec((ROW_TILE, COL_TILE), lambda i, j: (i, j))] + riders,
        out_shape=(jax.ShapeDtypeStruct((m, n), BF16),
                   jax.ShapeDtypeStruct(cast0.shape, BF16),
                   jax.ShapeDtypeStruct(cast1.shape, BF16)),
        compiler_params=_params("parallel", "arbitrary"),
        name="mlp_up",
    )(a, w, ssq, cast0, cast1)


def _mlp_down_kernel(a_hbm, w_hbm, r_ref, gain_ref, h_ref, hg_ref, ssq_ref,
                     a_ring, w_ring, sems, acc_ref, *, n_i, n_j, n_k):
    kk = pl.program_id(2)
    last = n_k - 1
    step = (pl.program_id(0) * n_j + pl.program_id(1)) * n_k + kk
    n_steps = n_i * n_j * n_k

    def tile_copies(s):
        slot = lax.rem(s, DOWN_RING)
        kk_s = lax.rem(s, n_k)
        ij_s = lax.div(s, n_k)
        rows = pl.ds(lax.div(ij_s, n_j) * ROW_TILE, ROW_TILE)
        cols = pl.ds(lax.rem(ij_s, n_j) * COL_TILE, COL_TILE)
        depth = pl.ds(kk_s * DOWN_K_TILE, DOWN_K_TILE)
        return (pltpu.make_async_copy(a_hbm.at[rows, depth], a_ring.at[slot], sems.at[0, slot]),
                pltpu.make_async_copy(w_hbm.at[depth, cols], w_ring.at[slot], sems.at[1, slot]))

    def start(s):
        for copy in tile_copies(s):
            copy.start()

    @pl.when(step == 0)
    def _():
        for s in range(RING_AHEAD):
            start(s)

    @pl.when(step + RING_AHEAD < n_steps)
    def _():
        start(step + RING_AHEAD)

    for copy in tile_copies(step):
        copy.wait()
    slot = lax.rem(step, DOWN_RING)

    def partial_product():
        return jnp.dot(a_ring[slot], w_ring[slot], preferred_element_type=F32)

    @pl.when(kk == 0)
    def _():
        acc_ref[...] = partial_product()

    @pl.when(jnp.logical_and(kk > 0, kk < last))
    def _():
        acc_ref[...] += partial_product()

    @pl.when(kk == last)
    def _():
        _store_residual(r_ref[...] + (acc_ref[...] + partial_product()), pl.program_id(1) == 0,
                        gain_ref, h_ref, hg_ref, ssq_ref)


def _mlp_down(a, w, r, next_gain):
    m, k = a.shape
    n = w.shape[1]
    n_i, n_j, n_k = m // ROW_TILE, n // COL_TILE, k // DOWN_K_TILE
    assert n_k >= 2, "first and last contraction steps must be distinct"
    assert n_i * n_j * n_k >= RING_AHEAD
    tile = pl.BlockSpec((ROW_TILE, COL_TILE), lambda i, j, kk: (i, j))
    return pl.pallas_call(
        functools.partial(_mlp_down_kernel, n_i=n_i, n_j=n_j, n_k=n_k),
        grid=(n_i, n_j, n_k),
        in_specs=[pl.BlockSpec(memory_space=pl.ANY),
                  pl.BlockSpec(memory_space=pl.ANY),
                  tile,
                  pl.BlockSpec((1, COL_TILE), lambda i, j, kk: (0, j))],
        out_specs=(tile, tile, pl.BlockSpec((ROW_TILE, 1), lambda i, j, kk: (i, 0))),
        out_shape=_residual_out(m, n),
        scratch_shapes=[pltpu.VMEM((DOWN_RING, ROW_TILE, DOWN_K_TILE), BF16),
                        pltpu.VMEM((DOWN_RING, DOWN_K_TILE, COL_TILE), BF16),
                        pltpu.SemaphoreType.DMA((2, DOWN_RING)),
                        pltpu.VMEM((ROW_TILE, COL_TILE), F32)],
        compiler_params=_params("arbitrary", "arbitrary", "arbitrary"),
        name="mlp_down",
    )(a, w, r, next_gain.reshape(1, n).astype(F32))


def _ple_kernel(a_ref, wg_ref, ssq_ref, p_ref, wp_ref, h_ref, gf_ref, o_ref, ssq_out_ref):
    j = pl.program_id(1)
    logits = jnp.dot(a_ref[...], wg_ref[...], preferred_element_type=F32)
    gate = 0.5 * jnp.tanh(logits * (0.5 * _row_inv_rms(ssq_ref, a_ref.shape[1]))) + 0.5
    e = jnp.dot(p_ref[...].astype(BF16), wp_ref[...], preferred_element_type=F32)
    h = h_ref[...] + gate * e
    bn = h.shape[1]
    col = _snake(pl.program_id(0), j, pl.num_programs(1))
    o_ref[:, pl.ds(pl.multiple_of(col * bn, bn), bn)] = h

    @pl.when(j == 0)
    def _():
        ssq_out_ref[...] = jnp.zeros_like(ssq_out_ref)

    ssq_out_ref[...] += jnp.sum(h * h, axis=-1, keepdims=True)

    @pl.when(j == pl.num_programs(1) - 1)
    def _():
        inv = _row_inv_rms(ssq_out_ref, o_ref.shape[1])
        for jj in range(o_ref.shape[1] // bn):
            cols = slice(jj * bn, (jj + 1) * bn)
            o_ref[:, cols] = o_ref[:, cols] * inv * gf_ref[:, cols]


def _ple_final(a, ssq, w_gate, p, w_proj, h, final_gain):
    m, k = a.shape
    n = w_gate.shape[1]
    kp = p.shape[1]
    n_j = n // PLE_COL_TILE
    return pl.pallas_call(
        _ple_kernel,
        grid=(m // PLE_ROW_TILE, n_j),
        in_specs=[pl.BlockSpec((PLE_ROW_TILE, k), lambda i, j: (i, 0),
                               pipeline_mode=pl.Buffered(1)),
                  pl.BlockSpec((k, PLE_COL_TILE), lambda i, j: (0, _snake(i, j, n_j))),
                  pl.BlockSpec((PLE_ROW_TILE, 1), lambda i, j: (i, 0)),
                  pl.BlockSpec((PLE_ROW_TILE, kp), lambda i, j: (i, 0)),
                  pl.BlockSpec((kp, PLE_COL_TILE), lambda i, j: (0, _snake(i, j, n_j))),
                  pl.BlockSpec((PLE_ROW_TILE, PLE_COL_TILE),
                               lambda i, j: (i, _snake(i, j, n_j))),
                  pl.BlockSpec((1, n), lambda i, j: (0, 0))],
        out_specs=pl.BlockSpec((PLE_ROW_TILE, n), lambda i, j: (i, 0)),
        out_shape=jax.ShapeDtypeStruct((m, n), F32),
        scratch_shapes=[pltpu.VMEM((PLE_ROW_TILE, 1), F32)],
        compiler_params=pltpu.CompilerParams(
            dimension_semantics=("arbitrary", "arbitrary"), vmem_limit_bytes=PLE_VMEM_LIMIT),
        name="ple_gate",
    )(a, w_gate, ssq, p, w_proj, h, final_gain.reshape(1, n).astype(F32))


def kernel(x, p, norm_mix, w_in, w_s, b_s, norm_a_out, norm_b_out, w_out, norm_ffn, w_up,
           w_down, norm_ple, w_ple_gate, w_ple_proj, norm_final):
    batch, seq, d_model = x.shape
    assert w_in.shape[0] == 1, "single layer: the gated-embedding kernel applies the final rmsnorm"
    a_width = norm_a_out.shape[1]
    b_width = norm_b_out.shape[1]
    sgu_heads = a_width // HEAD_DIM
    sb_heads = b_width // HEAD_DIM
    m = batch * seq
    scale = LOG2E / math.sqrt(HEAD_DIM)
    o1, o2 = a_width, 2 * a_width
    o3, o4 = o2 + b_width, o2 + 2 * b_width

    h = x.reshape(m, d_model)
    a = _rmsnorm(h, norm_mix[0], BF16)
    z, w_up_bf, w_out_bf = _in_proj(a, w_in[0].astype(BF16), o2, o3, scale, w_up[0], w_out[0])
    y_a = _sgu(z, w_s[0], b_s[0], norm_a_out[0], sgu_heads, 0, o1)
    y_b = _attention(z.reshape(batch, seq, -1), norm_b_out[0], sb_heads, o2, o3, o4)
    h, hg, ssq = _out_proj(y_a, y_b.reshape(m, b_width), w_out_bf, h, norm_ffn[0])
    hid, w_down_bf, w_gate_bf = _mlp_up(hg, ssq, w_up_bf, w_down[0], w_ple_gate[0])
    h, hg, ssq = _mlp_down(hid, w_down_bf, h, norm_ple[0])
    out = _ple_final(hg, ssq, w_gate_bf, p[0].reshape(m, PLE_DIM),
                     w_ple_proj[0].astype(BF16), h, norm_final)
    return out.reshape(batch, seq, d_model).astype(x.dtype)
```
